```python
import math
import jax, jax.numpy as jnp
from jax import lax
import numpy as np

D_MODEL = 1024
BATCH = 2
SEQ = 8192
DEPTH = 2
DEC_BATCH = 128
DEC_SEQ = 4
PAST_LEN = 16384
PAGE_SIZE = 128

N_MIXERS = 2
N_SSM_LAYERS = (DEPTH + 1) // 2
N_MLA_LAYERS = DEPTH // 2
SSM_GROUP = 16
SSM_WIDTH = D_MODEL
SSM_GROUPS = SSM_WIDTH // SSM_GROUP
SSM_STATE = 64
SSM_CHUNK = 128
DT_MIN = 1e-3
DT_MAX = 1e-1
MLA_HEADS = 16
Q_LORA = 384
KV_LORA = 256
QK_NOPE = 64
QK_ROPE = 32
QK_HEAD = QK_NOPE + QK_ROPE
V_HEAD = 64
ROPE_THETA = 10000.0
Q_BLOCK = 128
ATTN_SCALE = 1.0 / math.sqrt(QK_HEAD)
NEG_INF = -1e30
D_FF = 2816
N_ADA = 9
EPS = 1e-6
F32 = jnp.float32

kernel_name = "hybrid_s5_mla_macaron_adaln_step"


def rms_norm(x, g):
    xf = x.astype(F32)
    y = xf * lax.rsqrt(jnp.mean(xf * xf, axis=-1, keepdims=True) + EPS)
    return (y * g.astype(F32)).astype(x.dtype)


def ada_params(c, w, b):
    m = jax.nn.silu(c) @ w + b
    return m.reshape(c.shape[0], N_ADA, D_MODEL)


def sublayer_in(x, g, ada, j):
    shift, scale = ada[:, 3 * j], ada[:, 3 * j + 1]
    return rms_norm(x, g) * (1.0 + scale[:, None, :]) + shift[:, None, :]


def sublayer_out(x, ada, j, f_out, weight):
    return x + weight * ada[:, 3 * j + 2][:, None, :] * f_out


def swiglu(h, w_in, w_out):
    gate, up = jnp.split(h @ w_in, 2, axis=-1)
    return (jax.nn.silu(gate) * up) @ w_out


def _cmul(ar, ai, br, bi):
    return ar * br - ai * bi, ar * bi + ai * br


def _ssm_combine(e1, e2):
    a1r, a1i, b1r, b1i = e1
    a2r, a2i, b2r, b2i = e2
    ar, ai = _cmul(a2r, a2i, a1r, a1i)
    br, bi = _cmul(a2r, a2i, b1r, b1i)
    return ar, ai, br + b2r, bi + b2i


def s5_discretize(a_re, a_im, log_dt, b_re, b_im):
    dt = jnp.exp(log_dt)[:, None]
    mag = jnp.exp(a_re * dt)
    ab_re, ab_im = mag * jnp.cos(a_im * dt), mag * jnp.sin(a_im * dt)
    den = a_re * a_re + a_im * a_im
    n_re, n_im = ab_re - 1.0, ab_im
    w_re = (n_re * a_re + n_im * a_im) / den
    w_im = (n_im * a_re - n_re * a_im) / den
    bb_re = w_re[..., None] * b_re - w_im[..., None] * b_im
    bb_im = w_re[..., None] * b_im + w_im[..., None] * b_re
    return ab_re, ab_im, bb_re, bb_im


def s5_mixer(h, h0_re, h0_im, w_in, a_re, a_im, log_dt, b_re, b_im, c_re, c_im, d_skip, w_glu):
    bsz, t_len, _ = h.shape
    u = (h @ w_in).astype(F32).reshape(bsz, t_len, SSM_GROUPS, SSM_GROUP)
    ab_re, ab_im, bb_re, bb_im = s5_discretize(a_re.astype(F32), a_im.astype(F32), log_dt.astype(F32),
                                               b_re.astype(F32), b_im.astype(F32))
    cr, ci = c_re.astype(F32), c_im.astype(F32)
    dd = d_skip.astype(F32).reshape(SSM_GROUPS, SSM_GROUP)

    def block(carry, u_blk):
        hr0, hi0 = carry
        bu_re = jnp.einsum('btgc,gpc->btgp', u_blk, bb_re)
        bu_im = jnp.einsum('btgc,gpc->btgp', u_blk, bb_im)
        a_r = jnp.broadcast_to(ab_re, bu_re.shape)
        a_i = jnp.broadcast_to(ab_im, bu_im.shape)
        ac_r, ac_i, s_r, s_i = lax.associative_scan(_ssm_combine, (a_r, a_i, bu_re, bu_im), axis=1)
        pr, pi = _cmul(ac_r, ac_i, hr0[:, None], hi0[:, None])
        s_r, s_i = s_r + pr, s_i + pi
        y = (jnp.einsum('btgp,gcp->btgc', s_r, cr) - jnp.einsum('btgp,gcp->btgc', s_i, ci)
             + dd * u_blk)
        return (s_r[:, -1], s_i[:, -1]), y

    chunk = SSM_CHUNK if t_len % SSM_CHUNK == 0 else t_len
    n_chunks = t_len // chunk
    u_c = u.reshape(bsz, n_chunks, chunk, SSM_GROUPS, SSM_GROUP).transpose(1, 0, 2, 3, 4)
    (h_re, h_im), y = lax.scan(block, (h0_re.astype(F32), h0_im.astype(F32)), u_c)
    y = y.transpose(1, 0, 2, 3, 4).reshape(bsz, t_len, SSM_WIDTH)
    za, zb = jnp.split(jax.nn.gelu(y) @ w_glu.astype(F32), 2, axis=-1)
    return (za * jax.nn.sigmoid(zb)).astype(h.dtype), h_re, h_im


def rope_angles(pos):
    inv = 1.0 / (ROPE_THETA ** (jnp.arange(0, QK_ROPE, 2, dtype=F32) / QK_ROPE))
    ang = pos.astype(F32)[:, None] * inv[None, :]
    return jnp.cos(ang), jnp.sin(ang)


def apply_rope(x, cos, sin):
    x1, x2 = jnp.split(x.astype(F32), 2, axis=-1)
    c, s = cos[:, None, :], sin[:, None, :]
    return jnp.concatenate([x1 * c - x2 * s, x1 * s + x2 * c], axis=-1).astype(x.dtype)


def qk_finish(nope, rope, g, cos, sin):
    hn = rms_norm(jnp.concatenate([nope, rope], axis=-1), g)
    return jnp.concatenate([hn[..., :QK_NOPE], apply_rope(hn[..., QK_NOPE:], cos, sin)], axis=-1)


def mla_project(h, pos, w_a, g_qa, g_kva, w_uq, g_q):
    a = h @ w_a
    cq = rms_norm(a[..., :Q_LORA], g_qa)
    ckv = rms_norm(a[..., Q_LORA:Q_LORA + KV_LORA], g_kva)
    krope = a[..., Q_LORA + KV_LORA:]
    q = (cq @ w_uq).reshape(cq.shape[:-1] + (MLA_HEADS, QK_HEAD))
    cos, sin = rope_angles(pos)
    q = qk_finish(q[..., :QK_NOPE], q[..., QK_NOPE:], g_q, cos, sin)
    return q, ckv, krope


def mla_keys(ckv, krope, pos, w_ukv, g_k):
    kv = (ckv @ w_ukv).reshape(ckv.shape[:-1] + (MLA_HEADS, QK_NOPE + V_HEAD))
    k_nope, v = kv[..., :QK_NOPE], kv[..., QK_NOPE:]
    k_rope = jnp.broadcast_to(krope[..., None, :], krope.shape[:-1] + (MLA_HEADS, QK_ROPE))
    cos, sin = rope_angles(pos)
    return qk_finish(k_nope, k_rope, g_k, cos, sin), v


def attend(q, k, v, mask):
    s = jnp.einsum('bqhd,bshd->bhqs', q, k).astype(F32) * ATTN_SCALE
    s = jnp.where(mask[None, None], s, NEG_INF)
    p = jax.nn.softmax(s, axis=-1).astype(v.dtype)
    return jnp.einsum('bhqs,bshd->bqhd', p, v)


def mla_prompt_attention(q, k, v):
    bsz, t_len = q.shape[:2]
    k_pos = jnp.arange(t_len, dtype=jnp.int32)

    def blk(i):
        qb = lax.dynamic_slice_in_dim(q, i * Q_BLOCK, Q_BLOCK, axis=1)
        q_pos = i * Q_BLOCK + jnp.arange(Q_BLOCK, dtype=jnp.int32)
        return attend(qb, k, v, k_pos[None, :] <= q_pos[:, None])

    o = lax.map(blk, jnp.arange(t_len // Q_BLOCK))
    return o.transpose(1, 0, 2, 3, 4).reshape(bsz, t_len, MLA_HEADS * V_HEAD)


def mla_sample_attention(q, ckv_new, krope_new, cache_latent, cache_krope, layer, page_table, w_ukv, g_k):
    t_new = q.shape[1]
    n_past = page_table.shape[1] * PAGE_SIZE
    pos_all = jnp.arange(n_past + t_new, dtype=jnp.int32)
    q_pos = n_past + jnp.arange(t_new, dtype=jnp.int32)
    mask = pos_all[None, :] <= q_pos[:, None]

    def one(args):
        pt, q_seq, ckv_seq, kr_seq = args
        lat = cache_latent[layer, pt].reshape(n_past, KV_LORA)
        kr = cache_krope[layer, pt].reshape(n_past, QK_ROPE)
        lat = jnp.concatenate([lat, ckv_seq.astype(lat.dtype)], axis=0)
        kr = jnp.concatenate([kr, kr_seq.astype(kr.dtype)], axis=0)
        k, v = mla_keys(lat, kr, pos_all, w_ukv, g_k)
        return attend(q_seq[None], k[None], v[None], mask)[0]

    o = lax.map(one, (page_table, q, ckv_new, krope_new))
    return o.reshape(q.shape[0], t_new, MLA_HEADS * V_HEAD)


def setup_inputs(seed: int = 0) -> dict:
    key = jax.random.key(seed)
    ks = iter(jax.random.split(key, 48))

    def nrm(shape, scale):
        return jax.random.normal(next(ks), shape, F32) * scale

    def gain(shape):
        return 1.0 + nrm(shape, 0.05)

    n_pages = PAST_LEN // PAGE_SIZE
    n_used = DEC_BATCH * n_pages
    pool = n_used + max(1, n_used // 4)
    perm = jax.random.permutation(next(ks), pool)[:n_used]
    page_table = perm.reshape(DEC_BATCH, n_pages).astype(jnp.int32)

    a_im = (jnp.pi * jnp.arange(SSM_STATE, dtype=F32))[None, None, :] + nrm((N_SSM_LAYERS, SSM_GROUPS, SSM_STATE), 0.01)
    log_dt = jax.random.uniform(next(ks), (N_SSM_LAYERS, SSM_GROUPS), F32, math.log(DT_MIN), math.log(DT_MAX))

    return {
        "x_prompt": nrm((BATCH, SEQ, D_MODEL), 1.0),
        "x_sample": nrm((DEC_BATCH, DEC_SEQ, D_MODEL), 1.0),
        "c_prompt": nrm((BATCH, D_MODEL), 1.0),
        "c_sample": nrm((DEC_BATCH, D_MODEL), 1.0),
        "cache_latent": nrm((N_MLA_LAYERS, pool, PAGE_SIZE, KV_LORA), 1.0),
        "cache_krope": nrm((N_MLA_LAYERS, pool, PAGE_SIZE, QK_ROPE), 1.0),
        "state_ssm_re": nrm((N_SSM_LAYERS, DEC_BATCH, SSM_GROUPS, SSM_STATE), 0.1),
        "state_ssm_im": nrm((N_SSM_LAYERS, DEC_BATCH, SSM_GROUPS, SSM_STATE), 0.1),
        "page_table": page_table,
        "w_ada": nrm((DEPTH, D_MODEL, N_ADA * D_MODEL), 0.5 * D_MODEL ** -0.5),
        "b_ada": nrm((DEPTH, N_ADA * D_MODEL), 0.02),
        "norm_g": gain((DEPTH, 3, D_MODEL)),
        "ffn_w_in": nrm((DEPTH, 2, D_MODEL, 2 * D_FF), D_MODEL ** -0.5),
        "ffn_w_out": nrm((DEPTH, 2, D_FF, D_MODEL), D_FF ** -0.5),
        "ssm_w_in": nrm((N_SSM_LAYERS, D_MODEL, SSM_WIDTH), D_MODEL ** -0.5),
        "ssm_a_re": -0.5 * jnp.exp(nrm((N_SSM_LAYERS, SSM_GROUPS, SSM_STATE), 0.05)),
        "ssm_a_im": a_im,
        "ssm_log_dt": log_dt,
        "ssm_b_re": nrm((N_SSM_LAYERS, SSM_GROUPS, SSM_STATE, SSM_GROUP), (2 * SSM_GROUP) ** -0.5),
        "ssm_b_im": nrm((N_SSM_LAYERS, SSM_GROUPS, SSM_STATE, SSM_GROUP), (2 * SSM_GROUP) ** -0.5),
        "ssm_c_re": nrm((N_SSM_LAYERS, SSM_GROUPS, SSM_GROUP, SSM_STATE), (2 * SSM_STATE) ** -0.5),
        "ssm_c_im": nrm((N_SSM_LAYERS, SSM_GROUPS, SSM_GROUP, SSM_STATE), (2 * SSM_STATE) ** -0.5),
        "ssm_d": nrm((N_SSM_LAYERS, SSM_WIDTH), 1.0),
        "ssm_w_glu": nrm((N_SSM_LAYERS, SSM_WIDTH, 2 * D_MODEL), SSM_WIDTH ** -0.5),
        "mla_w_a": nrm((N_MLA_LAYERS, D_MODEL, Q_LORA + KV_LORA + QK_ROPE), D_MODEL ** -0.5),
        "mla_g_qa": gain((N_MLA_LAYERS, Q_LORA)),
        "mla_g_kva": gain((N_MLA_LAYERS, KV_LORA)),
        "mla_w_uq": nrm((N_MLA_LAYERS, Q_LORA, MLA_HEADS * QK_HEAD), Q_LORA ** -0.5),
        "mla_w_ukv": nrm((N_MLA_LAYERS, KV_LORA, MLA_HEADS * (QK_NOPE + V_HEAD)), KV_LORA ** -0.5),
        "mla_g_q": gain((N_MLA_LAYERS, QK_HEAD)),
        "mla_g_k": gain((N_MLA_LAYERS, QK_HEAD)),
        "mla_w_o": nrm((N_MLA_LAYERS, MLA_HEADS * V_HEAD, D_MODEL), (MLA_HEADS * V_HEAD) ** -0.5),
    }


def reference(x_prompt, x_sample, c_prompt, c_sample, cache_latent, cache_krope, state_ssm_re, state_ssm_im,
              page_table, w_ada, b_ada, norm_g, ffn_w_in, ffn_w_out, ssm_w_in, ssm_a_re, ssm_a_im, ssm_log_dt,
              ssm_b_re, ssm_b_im, ssm_c_re, ssm_c_im, ssm_d, ssm_w_glu, mla_w_a, mla_g_qa, mla_g_kva, mla_w_uq,
              mla_w_ukv, mla_g_q, mla_g_k, mla_w_o):
    xp, xs = x_prompt, x_sample
    pos_p = jnp.arange(xp.shape[1], dtype=jnp.int32)
    pos_s = page_table.shape[1] * PAGE_SIZE + jnp.arange(xs.shape[1], dtype=jnp.int32)
    lat_p, kr_p, lat_s, kr_s = [], [], [], []
    hre_p, him_p, hre_s, him_s = [], [], [], []

    for i in range(DEPTH):
        ada_p = ada_params(c_prompt, w_ada[i], b_ada[i])
        ada_s = ada_params(c_sample, w_ada[i], b_ada[i])
        xp = sublayer_out(xp, ada_p, 0, swiglu(sublayer_in(xp, norm_g[i, 0], ada_p, 0), ffn_w_in[i, 0], ffn_w_out[i, 0]), 0.5)
        xs = sublayer_out(xs, ada_s, 0, swiglu(sublayer_in(xs, norm_g[i, 0], ada_s, 0), ffn_w_in[i, 0], ffn_w_out[i, 0]), 0.5)
        hp = sublayer_in(xp, norm_g[i, 1], ada_p, 1)
        hs = sublayer_in(xs, norm_g[i, 1], ada_s, 1)
        j = i // N_MIXERS
        if i % N_MIXERS == 0:
            prm = (ssm_w_in[j], ssm_a_re[j], ssm_a_im[j], ssm_log_dt[j], ssm_b_re[j], ssm_b_im[j],
                   ssm_c_re[j], ssm_c_im[j], ssm_d[j], ssm_w_glu[j])
            z = jnp.zeros((xp.shape[0], SSM_GROUPS, SSM_STATE), F32)
            mp, hr, hi = s5_mixer(hp, z, z, *prm)
            ms, hr2, hi2 = s5_mixer(hs, state_ssm_re[j], state_ssm_im[j], *prm)
            hre_p.append(hr); him_p.append(hi); hre_s.append(hr2); him_s.append(hi2)
        else:
            q, ckv, kr = mla_project(hp, pos_p, mla_w_a[j], mla_g_qa[j], mla_g_kva[j], mla_w_uq[j], mla_g_q[j])
            k, v = mla_keys(ckv, kr, pos_p, mla_w_ukv[j], mla_g_k[j])
            mp = mla_prompt_attention(q, k, v) @ mla_w_o[j]
            q2, ckv2, kr2 = mla_project(hs, pos_s, mla_w_a[j], mla_g_qa[j], mla_g_kva[j], mla_w_uq[j], mla_g_q[j])
            ms = mla_sample_attention(q2, ckv2, kr2, cache_latent, cache_krope, j, page_table,
                                      mla_w_ukv[j], mla_g_k[j]) @ mla_w_o[j]
            lat_p.append(ckv); kr_p.append(kr); lat_s.append(ckv2); kr_s.append(kr2)
        xp = sublayer_out(xp, ada_p, 1, mp, 1.0)
        xs = sublayer_out(xs, ada_s, 1, ms, 1.0)
        xp = sublayer_out(xp, ada_p, 2, swiglu(sublayer_in(xp, norm_g[i, 2], ada_p, 2), ffn_w_in[i, 1], ffn_w_out[i, 1]), 0.5)
        xs = sublayer_out(xs, ada_s, 2, swiglu(sublayer_in(xs, norm_g[i, 2], ada_s, 2), ffn_w_in[i, 1], ffn_w_out[i, 1]), 0.5)

    return (xp, xs, jnp.stack(lat_p), jnp.stack(kr_p), jnp.stack(lat_s), jnp.stack(kr_s),
            jnp.stack(hre_p), jnp.stack(him_p), jnp.stack(hre_s), jnp.stack(him_s))
```

```python
import functools
import math

import jax
import jax.numpy as jnp
from jax import lax
from jax.experimental import pallas as pl
from jax.experimental.pallas import tpu as pltpu

F32 = jnp.float32
BF16 = jnp.bfloat16

EPS = 1e-6
NEG_INF = -1e30
N_ADA = 9
PAGE = 128
SSM_GROUP = 16
HEADS = 16
Q_LORA = 384
KV_LORA = 256
QK_NOPE = 64
QK_ROPE = 32
QK_HEAD = QK_NOPE + QK_ROPE
V_HEAD = 64
HEAD_PAD = 128
ROPE_THETA = 10000.0
ATTN_SCALE = 1.0 / math.sqrt(QK_HEAD)
DT_MIN = 1e-3
DT_MAX = 1e-1

LANES = 128
SUBLANES = 8
VMEM_LIMIT = 56 * 1024 * 1024

TOKEN_TILE = 512
S5_CHUNK = 256
S5_SEQS = 64
FLASH_TILE = 1024
PAGES_PER_STEP = 16


def _params(*sem):
    return pltpu.CompilerParams(dimension_semantics=sem, vmem_limit_bytes=VMEM_LIMIT)


def _dot(a, b):
    return jnp.dot(a, b, preferred_element_type=F32)


def _dot_nt(a, b):
    return lax.dot_general(a, b, (((1,), (1,)), ((), ())), preferred_element_type=F32)


def _mod_norm(x, g, scale, shift):
    ms = jnp.mean(x * x, axis=-1, keepdims=True)
    y = x * lax.rsqrt(ms + EPS) * g
    return y * (1.0 + scale) + shift


def _ada_kernel(c_ref, w_ref, b_ref, o_ref):
    c = c_ref[...]
    h = (c * jax.nn.sigmoid(c)).astype(BF16)
    o_ref[0] = _dot(h, w_ref[0].astype(BF16)) + b_ref[0]


def _ada_call(c_all, w_ada, b_ada):
    depth, d, n = w_ada.shape
    rows = c_all.shape[0]
    tn = 1152 if n % 1152 == 0 else n
    return pl.pallas_call(
        _ada_kernel,
        grid=(depth, n // tn),
        in_specs=[
            pl.BlockSpec((rows, d), lambda l, j: (0, 0)),
            pl.BlockSpec((1, d, tn), lambda l, j: (l, 0, j)),
            pl.BlockSpec((1, 1, tn), lambda l, j: (l, 0, j)),
        ],
        out_specs=pl.BlockSpec((1, rows, tn), lambda l, j: (l, 0, j)),
        out_shape=jax.ShapeDtypeStruct((depth, rows, n), F32),
        compiler_params=_params("arbitrary", "arbitrary"),
        name="ada",
    )(c_all, w_ada, b_ada.reshape(depth, 1, n))


def _mod_spec(mod, tiles_per_group):
    _, three, rows, d = mod.shape
    return pl.BlockSpec((1, three, rows, d), lambda i, *_: (i // tiles_per_group, 0, 0, 0))


def _tile(n, pref):
    return pref if n % pref == 0 else n


def _ffn_kernel(x_ref, g_ref, mod_ref, wg_ref, wu_ref, wo_ref, o_ref, h_scr, acc_scr):
    j = pl.program_id(1)

    @pl.when(j == 0)
    def _():
        h_scr[...] = _mod_norm(x_ref[...], g_ref[...], mod_ref[0, 1], mod_ref[0, 0]).astype(BF16)
        acc_scr[...] = jnp.zeros_like(acc_scr)

    h = h_scr[...]
    a = _dot(h, wg_ref[...])
    u = _dot(h, wu_ref[...])
    act = (a * jax.nn.sigmoid(a) * u).astype(BF16)
    acc_scr[...] += _dot(act, wo_ref[...])

    @pl.when(j == pl.num_programs(1) - 1)
    def _():
        o_ref[...] = x_ref[...] + (0.5 * mod_ref[0, 2]) * acc_scr[...]


def _ffn_call(x, g, mod, w_in, w_out, tiles_per_group, tm):
    n, d = x.shape
    f = w_out.shape[0]
    tf = f // 2 if (f // 2) % LANES == 0 else f
    nf = f // tf
    return pl.pallas_call(
        _ffn_kernel,
        grid=(n // tm, nf),
        in_specs=[
            pl.BlockSpec((tm, d), lambda i, j: (i, 0)),
            pl.BlockSpec((1, d), lambda i, j: (0, 0)),
            _mod_spec(mod, tiles_per_group),
            pl.BlockSpec((d, tf), lambda i, j: (0, j)),
            pl.BlockSpec((d, tf), lambda i, j: (0, nf + j)),
            pl.BlockSpec((tf, d), lambda i, j: (j, 0)),
        ],
        out_specs=pl.BlockSpec((tm, d), lambda i, j: (i, 0)),
        out_shape=jax.ShapeDtypeStruct((n, d), F32),
        scratch_shapes=[pltpu.VMEM((tm, d), BF16), pltpu.VMEM((tm, d), F32)],
        compiler_params=_params("arbitrary", "arbitrary"),
        name="ffn",
    )(x, g, mod, w_in, w_in, w_out)


def _s5_discretize(a_re, a_im, log_dt, b_re, b_im):
    dt = jnp.exp(log_dt)[:, None]
    mag = jnp.exp(a_re * dt)
    ab_re, ab_im = mag * jnp.cos(a_im * dt), mag * jnp.sin(a_im * dt)
    den = a_re * a_re + a_im * a_im
    n_re, n_im = ab_re - 1.0, ab_im
    w_re = (n_re * a_re + n_im * a_im) / den
    w_im = (n_im * a_re - n_re * a_im) / den
    bb_re = w_re[..., None] * b_re - w_im[..., None] * b_im
    bb_im = w_re[..., None] * b_im + w_im[..., None] * b_re
    return ab_re, ab_im, bb_re, bb_im


def _s5_weights(a_re, a_im, log_dt, b_re, b_im, c_re, c_im, d_skip):
    groups, state = a_re.shape
    ab_re, ab_im, bb_re, bb_im = _s5_discretize(a_re, a_im, log_dt, b_re, b_im)
    nkt = groups // 16
    eye16 = jnp.eye(16, dtype=F32)

    def bd(bb):
        t = bb.transpose(0, 2, 1).reshape(nkt, 16, SSM_GROUP, state)
        full = t[:, :, :, None, :] * eye16[None, :, None, :, None]
        return full.reshape(nkt, 16 * SSM_GROUP, 16 * state).astype(BF16)

    nj = groups // 8
    sel = (jnp.arange(16)[None, None, :] ==
           (jnp.arange(8)[None, :, None] + 8 * (jnp.arange(nj)[:, None, None] % 2))).astype(F32)

    def cd(cc):
        t = cc.transpose(0, 2, 1).reshape(nj, 8, state, SSM_GROUP)
        full = t[:, :, :, None, :] * sel[:, :, None, :, None]
        return full.reshape(nj, 8 * state, 16 * SSM_GROUP).astype(BF16)

    return dict(bd_re=bd(bb_re), bd_im=bd(bb_im), cd_re=cd(c_re), cd_im=cd(c_im),
                a_re=ab_re.reshape(1, groups * state), a_im=ab_im.reshape(1, groups * state),
                dd=d_skip.reshape(1, -1))


def _s5_tail(x, u, y_parts, dd, gate, wglu):
    y = jnp.concatenate(y_parts, axis=1) + dd * u
    z = _dot(jax.nn.gelu(y).astype(BF16), wglu)
    half = z.shape[1] // 2
    return x + gate * (z[:, :half] * jax.nn.sigmoid(z[:, half:]))


def _s5_prompt_kernel(x_ref, g_ref, mod_ref, win_ref, bdre_ref, bdim_ref, are_ref, aim_ref, cdre_ref, cdim_ref,
                      dd_ref, wglu_ref, o_ref, hre_ref, him_ref, sre_scr, sim_scr, cre_scr, cim_scr, *, t_len, pitch):
    @pl.when(pl.program_id(1) == 0)
    def _():
        cre_scr[...] = jnp.zeros_like(cre_scr)
        cim_scr[...] = jnp.zeros_like(cim_scr)

    x = x_ref[...]
    h = _mod_norm(x, g_ref[...], mod_ref[0, 1], mod_ref[0, 0]).astype(BF16)
    u = _dot(h, win_ref[...])
    ub = u.astype(BF16)
    nkt, kw, sw = bdre_ref.shape
    per_kt = sw // LANES
    nvreg = nkt * per_kt // SUBLANES
    for kt in range(nkt):
        ukt = ub[:, kt * kw:(kt + 1) * kw]
        bre = _dot(ukt, bdre_ref[kt])
        bim = _dot(ukt, bdim_ref[kt])
        for m in range(per_kt):
            r0 = (kt * per_kt + m) * pitch
            sre_scr[r0:r0 + t_len, :] = bre[:, m * LANES:(m + 1) * LANES]
            sim_scr[r0:r0 + t_len, :] = bim[:, m * LANES:(m + 1) * LANES]

    vrows = lambda k: slice(k * SUBLANES, (k + 1) * SUBLANES)
    a_re = [are_ref[vrows(k), :] for k in range(nvreg)]
    a_im = [aim_ref[vrows(k), :] for k in range(nvreg)]

    def step(t, carry):
        new = []
        for k in range(nvreg):
            sr, si = carry[2 * k], carry[2 * k + 1]
            rows = pl.ds(k * SUBLANES * pitch + t, SUBLANES, stride=pitch)
            nr = a_re[k] * sr - a_im[k] * si + sre_scr[rows, :]
            ni = a_re[k] * si + a_im[k] * sr + sim_scr[rows, :]
            sre_scr[rows, :] = nr
            sim_scr[rows, :] = ni
            new += [nr, ni]
        return tuple(new)

    init = []
    for k in range(nvreg):
        init += [cre_scr[vrows(k), :], cim_scr[vrows(k), :]]
    fin = lax.fori_loop(0, t_len, step, tuple(init), unroll=8)
    for k in range(nvreg):
        cre_scr[vrows(k), :] = fin[2 * k]
        cim_scr[vrows(k), :] = fin[2 * k + 1]
        hre_ref[0, vrows(k), :] = fin[2 * k]
        him_ref[0, vrows(k), :] = fin[2 * k + 1]

    y_parts = []
    for kt in range(nkt):
        slabs = lambda scr: jnp.concatenate(
            [scr[(kt * per_kt + m) * pitch:(kt * per_kt + m) * pitch + t_len, :] for m in range(per_kt)], axis=1)
        y_parts.append(_dot(slabs(sre_scr).astype(BF16), cdre_ref[kt]) - _dot(slabs(sim_scr).astype(BF16), cdim_ref[kt]))
    o_ref[...] = _s5_tail(x, u, y_parts, dd_ref[...], mod_ref[0, 2], wglu_ref[...])


def _s5_prompt_call(x, g, mod, w_in, sw, w_glu, bsz, t_len):
    n, d = x.shape
    seq = n // bsz
    nt = seq // t_len
    pitch = t_len + SUBLANES
    nstate = sw["a_re"].shape[1]
    nslab = nstate // LANES
    nkt = sw["bd_re"].shape[0]
    a_re = sw["a_re"].reshape(nslab, LANES)
    a_im = sw["a_im"].reshape(nslab, LANES)
    cd_re = sw["cd_re"].reshape(nkt, nstate // nkt, -1)
    cd_im = sw["cd_im"].reshape(nkt, nstate // nkt, -1)
    const = lambda *shape: pl.BlockSpec(shape, lambda b, t: (0,) * len(shape))
    out, hre, him = pl.pallas_call(
        functools.partial(_s5_prompt_kernel, t_len=t_len, pitch=pitch),
        grid=(bsz, nt),
        in_specs=[
            pl.BlockSpec((t_len, d), lambda b, t: (b * nt + t, 0)),
            const(1, d),
            pl.BlockSpec((1, 3, 1, d), lambda b, t: (b, 0, 0, 0)),
            const(*w_in.shape),
            const(*sw["bd_re"].shape), const(*sw["bd_im"].shape),
            const(nslab, LANES), const(nslab, LANES),
            const(*cd_re.shape), const(*cd_im.shape),
            const(1, d),
            const(*w_glu.shape),
        ],
        out_specs=[
            pl.BlockSpec((t_len, d), lambda b, t: (b * nt + t, 0)),
            pl.BlockSpec((1, nslab, LANES), lambda b, t: (b, 0, 0)),
            pl.BlockSpec((1, nslab, LANES), lambda b, t: (b, 0, 0)),
        ],
        out_shape=[jax.ShapeDtypeStruct((n, d), F32),
                   jax.ShapeDtypeStruct((bsz, nslab, LANES), F32),
                   jax.ShapeDtypeStruct((bsz, nslab, LANES), F32)],
        scratch_shapes=[pltpu.VMEM((nslab * pitch, LANES), F32), pltpu.VMEM((nslab * pitch, LANES), F32),
                        pltpu.VMEM((nslab, LANES), F32), pltpu.VMEM((nslab, LANES), F32)],
        compiler_params=_params("arbitrary", "arbitrary"),
        name="s5_prompt",
    )(x, g, mod, w_in, sw["bd_re"], sw["bd_im"], a_re, a_im, cd_re, cd_im, sw["dd"], w_glu)
    return out, hre, him


def _s5_sample_kernel(x_ref, g_ref, mod_ref, win_ref, bdre_ref, bdim_ref, are_ref, aim_ref, h0re_ref, h0im_ref,
                      cdre_ref, cdim_ref, dd_ref, wglu_ref, o_ref, hre_ref, him_ref, sre_scr, sim_scr):
    nt, bs, d = x_ref.shape
    x = x_ref[...].reshape(nt * bs, d)
    mods = [jnp.concatenate([mod_ref[0, k]] * nt, axis=0) for k in range(3)]
    h = _mod_norm(x, g_ref[...], mods[1], mods[0]).astype(BF16)
    u = _dot(h, win_ref[...])
    ub = u.astype(BF16)
    nkt = bdre_ref.shape[0]
    kw = bdre_ref.shape[1]
    sw = bdre_ref.shape[2]
    for kt in range(nkt):
        ukt = ub[:, kt * kw:(kt + 1) * kw]
        sre_scr[:, kt * sw:(kt + 1) * sw] = _dot(ukt, bdre_ref[kt])
        sim_scr[:, kt * sw:(kt + 1) * sw] = _dot(ukt, bdim_ref[kt])
    a_re = are_ref[...]
    a_im = aim_ref[...]
    sr = h0re_ref[...]
    si = h0im_ref[...]
    for t in range(nt):
        rows = slice(t * bs, (t + 1) * bs)
        nr = a_re * sr - a_im * si + sre_scr[rows, :]
        ni = a_re * si + a_im * sr + sim_scr[rows, :]
        sre_scr[rows, :] = nr
        sim_scr[rows, :] = ni
        sr, si = nr, ni
    hre_ref[...] = sr
    him_ref[...] = si
    y_parts = []
    for kt in range(nkt):
        cols = slice(kt * sw, (kt + 1) * sw)
        y_parts.append(_dot(sre_scr[:, cols].astype(BF16), cdre_ref[kt])
                       - _dot(sim_scr[:, cols].astype(BF16), cdim_ref[kt]))
    out = _s5_tail(x, u, y_parts, dd_ref[...], mods[2], wglu_ref[...])
    o_ref[...] = out.reshape(nt, bs, d)


def _s5_sample_call(x_tm, g, mod, w_in, sw, w_glu, h0_re, h0_im, bs):
    nt, nb, d = x_tm.shape
    nstate = sw["a_re"].shape[1]
    nkt = sw["bd_re"].shape[0]
    cd_re = sw["cd_re"].reshape(nkt, nstate // nkt, -1)
    cd_im = sw["cd_im"].reshape(nkt, nstate // nkt, -1)
    const = lambda *shape: pl.BlockSpec(shape, lambda i: (0,) * len(shape))
    return pl.pallas_call(
        _s5_sample_kernel,
        grid=(nb // bs,),
        in_specs=[
            pl.BlockSpec((nt, bs, d), lambda i: (0, i, 0)),
            const(1, d),
            pl.BlockSpec((1, 3, bs, d), lambda i: (i, 0, 0, 0)),
            const(*w_in.shape),
            const(*sw["bd_re"].shape), const(*sw["bd_im"].shape),
            const(1, nstate), const(1, nstate),
            pl.BlockSpec((bs, nstate), lambda i: (i, 0)),
            pl.BlockSpec((bs, nstate), lambda i: (i, 0)),
            const(*cd_re.shape), const(*cd_im.shape),
            const(1, d),
            const(*w_glu.shape),
        ],
        out_specs=[
            pl.BlockSpec((nt, bs, d), lambda i: (0, i, 0)),
            pl.BlockSpec((bs, nstate), lambda i: (i, 0)),
            pl.BlockSpec((bs, nstate), lambda i: (i, 0)),
        ],
        out_shape=[jax.ShapeDtypeStruct((nt, nb, d), F32),
                   jax.ShapeDtypeStruct((nb, nstate), F32),
                   jax.ShapeDtypeStruct((nb, nstate), F32)],
        scratch_shapes=[pltpu.VMEM((nt * bs, nstate), F32), pltpu.VMEM((nt * bs, nstate), F32)],
        compiler_params=_params("arbitrary"),
        name="s5_sample",
    )(x_tm, g, mod, w_in, sw["bd_re"], sw["bd_im"], sw["a_re"], sw["a_im"], h0_re, h0_im, cd_re, cd_im,
      sw["dd"], w_glu)


def _mla_weights(w_a, w_uq, w_ukv):
    d = w_a.shape[0]
    w_aq = w_a[:, :Q_LORA]
    w_akv = w_a[:, Q_LORA:Q_LORA + KV_LORA]
    w_akr = w_a[:, Q_LORA + KV_LORA:]
    kr1, kr2 = w_akr[:, :16], w_akr[:, 16:]
    z64, z32 = jnp.zeros((d, QK_NOPE), F32), jnp.zeros((d, 32), F32)
    wkr = jnp.concatenate([z64, kr1, kr2, z32], axis=1)
    wkr_sw = jnp.concatenate([z64, -kr2, kr1, z32], axis=1)
    wq = w_uq.reshape(Q_LORA, HEADS, QK_HEAD)
    qn, q1, q2 = wq[..., :QK_NOPE], wq[..., QK_NOPE:QK_NOPE + 16], wq[..., QK_NOPE + 16:]
    wuq = jnp.concatenate([qn, q1, q2, q2, q1], axis=-1).reshape(Q_LORA, HEADS * HEAD_PAD)
    wuq_sw = jnp.concatenate([jnp.zeros_like(qn), -q2, q1, q1, -q2], axis=-1).reshape(Q_LORA, HEADS * HEAD_PAD)
    wkv = w_ukv.reshape(KV_LORA, HEADS, QK_NOPE + V_HEAD)
    wk, wv = wkv[..., :QK_NOPE], wkv[..., QK_NOPE:]
    wuk = jnp.concatenate([wk, jnp.zeros_like(wk)], axis=-1).reshape(KV_LORA, HEADS * HEAD_PAD)
    wuv = wv.reshape(KV_LORA, HEADS * V_HEAD)
    wuk_t = wk.transpose(1, 2, 0)
    wabs = jnp.concatenate([wuk_t, jnp.zeros_like(wuk_t)], axis=1)
    eye_h = jnp.eye(HEADS, dtype=F32)
    wuv_bd = (wv.transpose(1, 0, 2)[:, :, None, :] * eye_h[:, None, :, None]).reshape(HEADS * KV_LORA, HEADS * V_HEAD)
    c = lambda w: w.astype(BF16)
    return dict(w_aq=c(w_aq), w_akv=c(w_akv), w_akr=c(w_akr), wkr=c(wkr), wkr_sw=c(wkr_sw), wuq=c(wuq),
                wuq_sw=c(wuq_sw), wuk=c(wuk), wuv=c(wuv), wuk_t=c(wuk_t.reshape(HEADS * QK_NOPE, KV_LORA)),
                wabs=c(wabs), wuv_bd=c(wuv_bd))


def _rope_cos_sin(pos):
    inv = 1.0 / (ROPE_THETA ** (jnp.arange(0, QK_ROPE, 2, dtype=F32) / QK_ROPE))
    ang = pos.astype(F32)[:, None] * inv[None, :]
    return jnp.cos(ang), jnp.sin(ang)


def _rope_tables(pos, g, scale, with_partner):
    c, s = _rope_cos_sin(pos)
    n = pos.shape[0]
    gn, g1, g2 = g[:QK_NOPE], g[QK_NOPE:QK_NOPE + 16], g[QK_NOPE + 16:]
    ones = jnp.ones((n, 1), F32)
    z16, z64 = jnp.zeros((n, 16), F32), jnp.zeros((n, QK_NOPE), F32)
    if with_partner:
        gc = jnp.concatenate([ones * gn, c * g1, c * g2, c * g2, -(c * g1)], axis=1)
        gs = jnp.concatenate([z64, s * g2, s * g1, s * g1, -(s * g2)], axis=1)
    else:
        gc = jnp.concatenate([ones * gn, c * g1, c * g2, z16, z16], axis=1)
        gs = jnp.concatenate([z64, s * g2, s * g1, z16, z16], axis=1)
    return gc * scale, gs * scale


def _mla_proj_kernel(x_ref, g_ref, mod_ref, waq_ref, wakv_ref, wakr_ref, wkr_ref, wkrsw_ref, gqa_ref, gkva_ref,
                     wuq_ref, wuqsw_ref, wuk_ref, wuv_ref, gcq_ref, gsq_ref, gck_ref, gsk_ref,
                     q_ref, k_ref, v_ref, ckv_ref, kr_ref, *, with_kv):
    x = x_ref[...]
    h = _mod_norm(x, g_ref[...], mod_ref[0, 1], mod_ref[0, 0]).astype(BF16)
    aq = _dot(h, waq_ref[...])
    akv = _dot(h, wakv_ref[...])
    kr_ref[...] = _dot(h, wakr_ref[...])
    cq = (aq * lax.rsqrt(jnp.mean(aq * aq, axis=-1, keepdims=True) + EPS) * gqa_ref[...]).astype(BF16)
    ckv = akv * lax.rsqrt(jnp.mean(akv * akv, axis=-1, keepdims=True) + EPS) * gkva_ref[...]
    ckv_ref[...] = ckv
    q_pre = _dot(cq, wuq_ref[...])
    q_sw = _dot(cq, wuqsw_ref[...])
    tm = x.shape[0]
    in_head = lax.broadcasted_iota(jnp.int32, (tm, HEAD_PAD), 1) < QK_HEAD
    gcq, gsq = gcq_ref[...], gsq_ref[...]
    for hd in range(HEADS):
        cols = slice(hd * HEAD_PAD, (hd + 1) * HEAD_PAD)
        qp = q_pre[:, cols]
        ssq = jnp.sum(jnp.where(in_head, qp * qp, 0.0), axis=-1, keepdims=True)
        rinv = lax.rsqrt(ssq * (1.0 / QK_HEAD) + EPS)
        q_ref[:, cols] = (rinv * (qp * gcq + q_sw[:, cols] * gsq)).astype(q_ref.dtype)
    if with_kv:
        ckv_b = ckv.astype(BF16)
        k_pre = _dot(ckv_b, wuk_ref[...])
        v_ref[...] = _dot(ckv_b, wuv_ref[...]).astype(v_ref.dtype)
        kr_slab = _dot(h, wkr_ref[...])
        kr_sw = _dot(h, wkrsw_ref[...])
        gck = gck_ref[...]
        kr_rot = kr_sw * gsk_ref[...]
        for hd in range(HEADS):
            cols = slice(hd * HEAD_PAD, (hd + 1) * HEAD_PAD)
            kp = k_pre[:, cols] + kr_slab
            ssq = jnp.sum(kp * kp, axis=-1, keepdims=True)
            rinv = lax.rsqrt(ssq * (1.0 / QK_HEAD) + EPS)
            k_ref[:, cols] = (rinv * (kp * gck + kr_rot)).astype(k_ref.dtype)
    else:
        k_ref[...] = jnp.zeros_like(k_ref)
        v_ref[...] = jnp.zeros_like(v_ref)


def _mla_proj_call(x, g, mod, mw, g_qa, g_kva, tabs, tiles_per_group, tm, with_kv, q_dtype):
    n, d = x.shape
    gcq, gsq, gck, gsk = tabs
    ntab = gcq.shape[0] // tm
    const = lambda a: pl.BlockSpec(a.shape, lambda i: (0,) * a.ndim)
    tab = pl.BlockSpec((tm, HEAD_PAD), lambda i: (i % ntab, 0))
    row = lambda w: pl.BlockSpec((tm, w), lambda i: (i, 0))
    kv_rows = tm if with_kv else SUBLANES
    kv_n = n if with_kv else SUBLANES * (n // tm)
    kvrow = lambda w: pl.BlockSpec((kv_rows, w), lambda i: (i, 0))
    g_qa2, g_kva2 = g_qa.reshape(1, -1), g_kva.reshape(1, -1)
    ws = [mw["w_aq"], mw["w_akv"], mw["w_akr"], mw["wkr"], mw["wkr_sw"], g_qa2, g_kva2,
          mw["wuq"], mw["wuq_sw"], mw["wuk"], mw["wuv"]]
    return pl.pallas_call(
        functools.partial(_mla_proj_kernel, with_kv=with_kv),
        grid=(n // tm,),
        in_specs=[row(d), pl.BlockSpec((1, d), lambda i: (0, 0)), _mod_spec(mod, tiles_per_group)]
                 + [const(w) for w in ws] + [tab, tab, tab, tab],
        out_specs=[row(HEADS * HEAD_PAD), kvrow(HEADS * HEAD_PAD), kvrow(HEADS * V_HEAD), row(KV_LORA),
                   row(QK_ROPE)],
        out_shape=[jax.ShapeDtypeStruct((n, HEADS * HEAD_PAD), q_dtype),
                   jax.ShapeDtypeStruct((kv_n, HEADS * HEAD_PAD), BF16),
                   jax.ShapeDtypeStruct((kv_n, HEADS * V_HEAD), BF16),
                   jax.ShapeDtypeStruct((n, KV_LORA), F32),
                   jax.ShapeDtypeStruct((n, QK_ROPE), F32)],
        compiler_params=_params("arbitrary"),
        name="mla_proj",
    )(x, g, mod, *ws, gcq, gsq, gck, gsk)


def _flash_kernel(qi_ref, ki_ref, q_ref, k_ref, v_ref, o_ref, m_scr, l_scr, acc_scr):
    p_id = pl.program_id(2)
    qi, ki = qi_ref[p_id], ki_ref[p_id]
    tq, tk = q_ref.shape[0], k_ref.shape[0]

    @pl.when(ki == 0)
    def _():
        m_scr[...] = jnp.full_like(m_scr, NEG_INF)
        l_scr[...] = jnp.zeros_like(l_scr)
        acc_scr[...] = jnp.zeros_like(acc_scr)

    row = qi * tq + lax.broadcasted_iota(jnp.int32, (tq, tk), 0)
    col = ki * tk + lax.broadcasted_iota(jnp.int32, (tq, tk), 1)
    visible = col <= row
    v = v_ref[...]
    for hh in range(2):
        cols = slice(hh * HEAD_PAD, (hh + 1) * HEAD_PAD)
        s = _dot_nt(q_ref[:, cols], k_ref[:, cols])
        s = jnp.where(visible, s, NEG_INF)
        m_prev = m_scr[hh]
        m_new = jnp.maximum(m_prev, jnp.max(s, axis=1, keepdims=True))
        alpha = jnp.exp(m_prev - m_new)
        p = jnp.exp(s - m_new)
        l_scr[hh] = alpha * l_scr[hh] + jnp.sum(p, axis=1, keepdims=True)
        acc_scr[hh] = alpha * acc_scr[hh] + _dot(p.astype(BF16), v)
        m_scr[hh] = m_new

    @pl.when(ki == qi)
    def _():
        lane = lax.broadcasted_iota(jnp.int32, (tq, 2 * V_HEAD), 1)
        o_ref[...] = jnp.where(lane < V_HEAD, acc_scr[0] / l_scr[0], acc_scr[1] / l_scr[1]).astype(o_ref.dtype)


def _flash_call(q, k, v, bsz, tq):
    n = q.shape[0]
    seq = n // bsz
    nq = seq // tq
    pairs = [(a, b) for a in range(nq) for b in range(a + 1)]
    qi_tab = jnp.asarray([p[0] for p in pairs], jnp.int32)
    ki_tab = jnp.asarray([p[1] for p in pairs], jnp.int32)
    grid_spec = pltpu.PrefetchScalarGridSpec(
        num_scalar_prefetch=2,
        grid=(bsz, HEADS // 2, len(pairs)),
        in_specs=[
            pl.BlockSpec((tq, 2 * HEAD_PAD), lambda b, hp, p, qi, ki: (b * nq + qi[p], hp)),
            pl.BlockSpec((tq, 2 * HEAD_PAD), lambda b, hp, p, qi, ki: (b * nq + ki[p], hp)),
            pl.BlockSpec((tq, 2 * V_HEAD), lambda b, hp, p, qi, ki: (b * nq + ki[p], hp)),
        ],
        out_specs=pl.BlockSpec((tq, 2 * V_HEAD), lambda b, hp, p, qi, ki: (b * nq + qi[p], hp)),
        scratch_shapes=[pltpu.VMEM((2, tq, 1), F32), pltpu.VMEM((2, tq, 1), F32),
                        pltpu.VMEM((2, tq, 2 * V_HEAD), F32)],
    )
    return pl.pallas_call(
        _flash_kernel,
        grid_spec=grid_spec,
        out_shape=jax.ShapeDtypeStruct((n, HEADS * V_HEAD), BF16),
        compiler_params=_params("arbitrary", "arbitrary", "arbitrary"),
        name="flash_prompt",
    )(qi_tab, ki_tab, q, k, v)


def _absorb_kernel(q_ref, gk_ref, wabs_ref, o_ref):
    gk = gk_ref[...]
    for hd in range(HEADS):
        qn = (q_ref[:, hd * HEAD_PAD:(hd + 1) * HEAD_PAD] * gk).astype(BF16)
        o_ref[:, hd * KV_LORA:(hd + 1) * KV_LORA] = _dot(qn, wabs_ref[hd]).astype(o_ref.dtype)


def _absorb_call(q, gk_slab, wabs):
    n = q.shape[0]
    return pl.pallas_call(
        _absorb_kernel,
        grid=(1,),
        in_specs=[pl.BlockSpec(q.shape, lambda i: (0, 0)), pl.BlockSpec(gk_slab.shape, lambda i: (0, 0)),
                  pl.BlockSpec(wabs.shape, lambda i: (0, 0, 0))],
        out_specs=pl.BlockSpec((n, HEADS * KV_LORA), lambda i: (0, 0)),
        out_shape=jax.ShapeDtypeStruct((n, HEADS * KV_LORA), BF16),
        compiler_params=_params("arbitrary"),
        name="absorb_q",
    )(q, gk_slab, wabs)


def _attend_chunk(lhs, qx, latb, kr, g4, mask, m_scr, l_scr, acc_scr):
    n = latb.shape[0]
    nk = HEADS * QK_NOPE
    kq = _dot_nt(lhs, latb)
    kn = kq[:nk]
    ssq = jnp.sum((kn * kn).reshape(HEADS, QK_NOPE, n), axis=1)
    kr4 = jnp.concatenate([kr, kr, kr, kr], axis=1)
    lane = lax.broadcasted_iota(jnp.int32, (n, LANES), 1)
    sq = kr4 * kr4
    sq_lo = sq - sq.astype(BF16).astype(F32)
    xr = jnp.where(lane < 2 * QK_ROPE, kr4 * g4, jnp.where(lane < 3 * QK_ROPE, sq, sq_lo)).astype(BF16)
    kx = _dot_nt(qx, xr)
    nrow = lhs.shape[0] - nk
    ssq = ssq + kx[nrow:nrow + 1]
    rinv = lax.rsqrt(ssq * (1.0 / QK_HEAD) + EPS)
    s = (kq[nk:] + kx[:nrow]) * jnp.concatenate([rinv] * (nrow // HEADS), axis=0)
    if mask is not None:
        s = jnp.where(mask, s, NEG_INF)
    m_prev = m_scr[...]
    m_new = jnp.maximum(m_prev, jnp.max(s, axis=1, keepdims=True))
    alpha = jnp.exp(m_prev - m_new)
    p = jnp.exp(s - m_new)
    l_scr[...] = alpha * l_scr[...] + jnp.sum(p, axis=1, keepdims=True)
    acc_scr[...] = alpha * acc_scr[...] + _dot(p.astype(BF16), latb)
    m_scr[...] = m_new


def _sample_attn_kernel(pt_ref, qa_ref, qx_ref, wukt_ref, *rest, pages, chunk, n_new):
    lat_refs = rest[:pages]
    kr_refs = rest[pages:2 * pages]
    g4_ref, latn_ref, krn_ref, g4n_ref, o_ref, lhs_scr, lat_scr, kr_scr, m_scr, l_scr, acc_scr = rest[2 * pages:]
    j = pl.program_id(1)
    nk = wukt_ref.shape[0]
    nrow = qa_ref.shape[0]

    @pl.when(j == 0)
    def _():
        lhs_scr[0:nk, :] = wukt_ref[...]
        lhs_scr[nk:nk + nrow, :] = qa_ref[...]
        m_scr[...] = jnp.full_like(m_scr, NEG_INF)
        l_scr[...] = jnp.zeros_like(l_scr)
        acc_scr[...] = jnp.zeros_like(acc_scr)

    for k in range(pages):
        lat_scr[k * PAGE:(k + 1) * PAGE, :] = lat_refs[k][...].astype(BF16)
        kr_scr[k * PAGE:(k + 1) * PAGE, :] = kr_refs[k][...]
    lhs = lhs_scr[...]
    qx = qx_ref[...]
    for c in range(pages * PAGE // chunk):
        rows = slice(c * chunk, (c + 1) * chunk)
        _attend_chunk(lhs, qx, lat_scr[rows, :], kr_scr[rows, :], g4_ref[rows, :], None, m_scr, l_scr, acc_scr)

    @pl.when(j == pl.num_programs(1) - 1)
    def _():
        t_row = lax.broadcasted_iota(jnp.int32, (nrow, PAGE), 0) // HEADS
        kk = lax.broadcasted_iota(jnp.int32, (nrow, PAGE), 1)
        mask = (kk < n_new) & (kk <= t_row)
        _attend_chunk(lhs, qx, latn_ref[...].astype(BF16), krn_ref[...], g4n_ref[...], mask, m_scr, l_scr, acc_scr)
        o_ref[...] = acc_scr[...] / l_scr[...]


def _sample_attn_call(page_table, qabs, qx, wuk_t, cache_latent, cache_krope, layer, g4, lat_new, kr_new, g4_new,
                      n_new, pages):
    nb, n_pages = page_table.shape
    nj = n_pages // pages
    nrow = qabs.shape[1]
    chunk = min(pages * PAGE, 1024)
    pt_flat = page_table.reshape(-1)

    def page_spec(width, k):
        return pl.BlockSpec((None, None, PAGE, width),
                            lambda b, j, pt: (layer, pt[b * n_pages + j * pages + k], 0, 0))

    grid_spec = pltpu.PrefetchScalarGridSpec(
        num_scalar_prefetch=1,
        grid=(nb, nj),
        in_specs=[
            pl.BlockSpec((None, nrow, KV_LORA), lambda b, j, pt: (b, 0, 0)),
            pl.BlockSpec((None,) + qx.shape[1:], lambda b, j, pt: (b, 0, 0)),
            pl.BlockSpec(wuk_t.shape, lambda b, j, pt: (0, 0)),
        ] + [page_spec(KV_LORA, k) for k in range(pages)] + [page_spec(QK_ROPE, k) for k in range(pages)] + [
            pl.BlockSpec((pages * PAGE, LANES), lambda b, j, pt: (j, 0)),
            pl.BlockSpec((None, PAGE, KV_LORA), lambda b, j, pt: (b, 0, 0)),
            pl.BlockSpec((None, PAGE, QK_ROPE), lambda b, j, pt: (b, 0, 0)),
            pl.BlockSpec((PAGE, LANES), lambda b, j, pt: (0, 0)),
        ],
        out_specs=pl.BlockSpec((None, nrow, KV_LORA), lambda b, j, pt: (b, 0, 0)),
        scratch_shapes=[pltpu.VMEM((wuk_t.shape[0] + nrow, KV_LORA), BF16),
                        pltpu.VMEM((pages * PAGE, KV_LORA), BF16),
                        pltpu.VMEM((pages * PAGE, QK_ROPE), F32),
                        pltpu.VMEM((nrow, 1), F32), pltpu.VMEM((nrow, 1), F32),
                        pltpu.VMEM((nrow, KV_LORA), F32)],
    )
    return pl.pallas_call(
        functools.partial(_sample_attn_kernel, pages=pages, chunk=chunk, n_new=n_new),
        grid_spec=grid_spec,
        out_shape=jax.ShapeDtypeStruct((nb, nrow, KV_LORA), F32),
        compiler_params=_params("arbitrary", "arbitrary"),
        name="sample_attn",
    )(pt_flat, qabs, qx, wuk_t, *([cache_latent] * pages), *([cache_krope] * pages), g4, lat_new, kr_new, g4_new)


def _attn_out_kernel(x_ref, a_ref, mod_ref, *rest, pre_proj):
    if pre_proj:
        wuv_ref, wo_ref, o_ref = rest
        a = _dot(a_ref[...].astype(BF16), wuv_ref[...]).astype(BF16)
    else:
        wo_ref, o_ref = rest
        a = a_ref[...].astype(BF16)
    o_ref[...] = x_ref[...] + mod_ref[0, 2] * _dot(a, wo_ref[...])


def _attn_out_call(x, a, mod, w_o, tiles_per_group, tm, wuv_bd=None):
    n, d = x.shape
    ws = [w_o] if wuv_bd is None else [wuv_bd, w_o]
    return pl.pallas_call(
        functools.partial(_attn_out_kernel, pre_proj=wuv_bd is not None),
        grid=(n // tm,),
        in_specs=[pl.BlockSpec((tm, d), lambda i: (i, 0)), pl.BlockSpec((tm, a.shape[1]), lambda i: (i, 0)),
                  _mod_spec(mod, tiles_per_group)] + [pl.BlockSpec(w.shape, lambda i: (0, 0)) for w in ws],
        out_specs=pl.BlockSpec((tm, d), lambda i: (i, 0)),
        out_shape=jax.ShapeDtypeStruct((n, d), F32),
        compiler_params=_params("arbitrary"),
        name="attn_out",
    )(x, a, mod, *ws)


def kernel(x_prompt, x_sample, c_prompt, c_sample, cache_latent, cache_krope, state_ssm_re, state_ssm_im, page_table, w_ada, b_ada, norm_g, ffn_w_in, ffn_w_out, ssm_w_in, ssm_a_re, ssm_a_im, ssm_log_dt, ssm_b_re, ssm_b_im, ssm_c_re, ssm_c_im, ssm_d, ssm_w_glu, mla_w_a, mla_g_qa, mla_g_kva, mla_w_uq, mla_w_ukv, mla_g_q, mla_g_k, mla_w_o):
    bsz, seq, d = x_prompt.shape
    nb, nt, _ = x_sample.shape
    depth = w_ada.shape[0]
    n_past = page_table.shape[1] * PAGE

    tm_p = _tile(seq, TOKEN_TILE)
    tpg_p = seq // tm_p
    n_s = nb * nt
    s5_t = _tile(seq, S5_CHUNK)
    s5_bs = _tile(nb, S5_SEQS)
    tq = _tile(seq, FLASH_TILE)
    pages = _tile(page_table.shape[1], PAGES_PER_STEP)

    c_all = jnp.concatenate([c_prompt, c_sample], axis=0)
    pad = (-c_all.shape[0]) % SUBLANES
    ada = _ada_call(jnp.pad(c_all, ((0, pad), (0, 0))), w_ada, b_ada)
    ada = ada.reshape(depth, -1, N_ADA, d)

    xp = x_prompt.reshape(bsz * seq, d)
    xs = x_sample.transpose(1, 0, 2).reshape(n_s, d)

    pos_p = jnp.arange(seq, dtype=jnp.int32)
    pos_s = n_past + jnp.arange(nt, dtype=jnp.int32)
    outs = {k: [] for k in ("lat_p", "kr_p", "lat_s", "kr_s", "hre_p", "him_p", "hre_s", "him_s")}

    for i in range(depth):
        ada_p = ada[i, :bsz]
        ada_s = ada[i, bsz:bsz + nb]
        mod_p = [ada_p[:, 3 * j:3 * j + 3, None, :] for j in range(3)]
        mod_s_seq = [ada_s[:, 3 * j:3 * j + 3].transpose(1, 0, 2) for j in range(3)]
        mod_s = [jnp.tile(m, (1, nt, 1))[None] for m in mod_s_seq]
        g = norm_g[i].reshape(3, 1, d)
        w_in = ffn_w_in[i].astype(BF16)
        w_out = ffn_w_out[i].astype(BF16)

        xp = _ffn_call(xp, g[0], mod_p[0], w_in[0], w_out[0], tpg_p, tm_p)
        xs = _ffn_call(xs, g[0], mod_s[0], w_in[0], w_out[0], 1, n_s)

        j = i // 2
        if i % 2 == 0:
            sw = _s5_weights(ssm_a_re[j], ssm_a_im[j], ssm_log_dt[j], ssm_b_re[j], ssm_b_im[j],
                             ssm_c_re[j], ssm_c_im[j], ssm_d[j])
            w_sin = ssm_w_in[j].astype(BF16)
            w_glu = ssm_w_glu[j].astype(BF16)
            xp, hre, him = _s5_prompt_call(xp, g[1], mod_p[1], w_sin, sw, w_glu, bsz, s5_t)
            groups, state = ssm_a_re[j].shape
            outs["hre_p"].append(hre.reshape(bsz, groups, state))
            outs["him_p"].append(him.reshape(bsz, groups, state))
            mod_blk = mod_s_seq[1].reshape(3, nb // s5_bs, s5_bs, d).transpose(1, 0, 2, 3)
            xs3, hre, him = _s5_sample_call(xs.reshape(nt, nb, d), g[1], mod_blk, w_sin, sw, w_glu,
                                            state_ssm_re[j].reshape(nb, -1), state_ssm_im[j].reshape(nb, -1), s5_bs)
            xs = xs3.reshape(n_s, d)
            outs["hre_s"].append(hre.reshape(nb, groups, state))
            outs["him_s"].append(him.reshape(nb, groups, state))
        else:
            mw = _mla_weights(mla_w_a[j], mla_w_uq[j], mla_w_ukv[j])
            w_o = mla_w_o[j].astype(BF16)
            g_q, g_k = mla_g_q[j], mla_g_k[j]
            tabs_p = _rope_tables(pos_p, g_q, ATTN_SCALE, False) + _rope_tables(pos_p, g_k, 1.0, False)
            q, k, v, ckv, kr = _mla_proj_call(xp, g[1], mod_p[1], mw, mla_g_qa[j], mla_g_kva[j], tabs_p, tpg_p, tm_p,
                                              True, BF16)
            outs["lat_p"].append(ckv.reshape(bsz, seq, KV_LORA))
            outs["kr_p"].append(kr.reshape(bsz, seq, QK_ROPE))
            attn = _flash_call(q, k, v, bsz, tq)
            xp = _attn_out_call(xp, attn, mod_p[1], w_o, tpg_p, tm_p)
            pos_rows = jnp.repeat(pos_s, nb)
            tabs_s = _rope_tables(pos_rows, g_q, ATTN_SCALE, True) + _rope_tables(pos_rows, g_k, 1.0, False)
            q_s, _, _, ckv_s, kr_s = _mla_proj_call(xs, g[1], mod_s[1], mw, mla_g_qa[j], mla_g_kva[j], tabs_s, 1, n_s,
                                                    False, F32)
            lat_new = ckv_s.reshape(nt, nb, KV_LORA).transpose(1, 0, 2)
            kr_new = kr_s.reshape(nt, nb, QK_ROPE).transpose(1, 0, 2)
            outs["lat_s"].append(lat_new)
            outs["kr_s"].append(kr_new)
            q_sm = q_s.reshape(nt, nb, HEADS * HEAD_PAD).transpose(1, 0, 2).reshape(n_s, HEADS * HEAD_PAD)
            gk_slab = jnp.concatenate([g_k[:QK_NOPE], jnp.zeros((HEAD_PAD - QK_NOPE,), F32)]).reshape(1, HEAD_PAD)
            qabs = _absorb_call(q_sm, gk_slab, mw["wabs"]).reshape(nb, nt * HEADS, KV_LORA)
            q_rope = q_sm.reshape(nb, nt * HEADS, HEAD_PAD)[..., QK_NOPE:]
            qx = jnp.concatenate([q_rope, jnp.zeros_like(q_rope)], axis=-1)
            ones_rows = jnp.concatenate([jnp.zeros((nb, SUBLANES, 2 * QK_ROPE), F32),
                                         jnp.ones((nb, SUBLANES, 2 * QK_ROPE), F32)], axis=-1)
            qx = jnp.concatenate([qx, ones_rows], axis=1).astype(BF16)
            pos_k = jnp.arange(n_past + PAGE, dtype=jnp.int32)
            ck, sk = _rope_cos_sin(pos_k)
            g1, g2 = g_k[QK_NOPE:QK_NOPE + 16], g_k[QK_NOPE + 16:]
            g4 = jnp.concatenate([ck * g1, ck * g2, sk * g1, sk * g2, jnp.zeros((n_past + PAGE, 2 * QK_ROPE), F32)],
                                 axis=1)
            padn = ((0, 0), (0, PAGE - nt), (0, 0))
            ctx = _sample_attn_call(page_table, qabs, qx, mw["wuk_t"], cache_latent, cache_krope, j, g4[:n_past],
                                    jnp.pad(lat_new, padn), jnp.pad(kr_new, padn), g4[n_past:], nt, pages)
            ctx_tm = ctx.reshape(nb, nt, HEADS * KV_LORA).transpose(1, 0, 2).reshape(n_s, HEADS * KV_LORA)
            xs = _attn_out_call(xs, ctx_tm, mod_s[1], w_o, 1, n_s, wuv_bd=mw["wuv_bd"])

        xp = _ffn_call(xp, g[2], mod_p[2], w_in[1], w_out[1], tpg_p, tm_p)
        xs = _ffn_call(xs, g[2], mod_s[2], w_in[1], w_out[1], 1, n_s)

    y_prompt = xp.reshape(bsz, seq, d)
    y_sample = xs.reshape(nt, nb, d).transpose(1, 0, 2)
    st = lambda k: jnp.stack(outs[k])
    return (y_prompt, y_sample, st("lat_p"), st("kr_p"), st("lat_s"), st("kr_s"),
            st("hre_p"), st("him_p"), st("hre_s"), st("him_s"))
```

```python
import functools
import math

import jax
import jax.numpy as jnp
from jax import lax
from jax.experimental import pallas as pl
from jax.experimental.pallas import tpu as pltpu

F32 = jnp.float32
BF16 = jnp.bfloat16

EPS = 1e-6
NEG_INF = -1e30
N_ADA = 9
PAGE = 128
SSM_GROUP = 16
HEADS = 16
Q_LORA = 384
KV_LORA = 256
QK_NOPE = 64
QK_ROPE = 32
QK_HEAD = QK_NOPE + QK_ROPE
V_HEAD = 64
HEAD_PAD = 128
ROPE_THETA = 10000.0
ATTN_SCALE = 1.0 / math.sqrt(QK_HEAD)
LOG2E = math.log2(math.e)
DT_MIN = 1e-3
DT_MAX = 1e-1

LANES = 128
SUBLANES = 8
VMEM_LIMIT = 56 * 1024 * 1024

TOKEN_TILE = 512
S5_CHUNK = 256
S5_SEQS = 64
FLASH_TILE = 1024
FLASH_ROWS = 256
PAGES_PER_STEP = 16
SAMPLE_CHUNK = 512


def _params(*sem):
    return pltpu.CompilerParams(dimension_semantics=sem, vmem_limit_bytes=VMEM_LIMIT)


def _dot(a, b):
    return jnp.dot(a, b, preferred_element_type=F32)


def _dot_nt(a, b):
    return lax.dot_general(a, b, (((1,), (1,)), ((), ())), preferred_element_type=F32)


def _mod_norm(x, g, scale, shift):
    ms = jnp.mean(x * x, axis=-1, keepdims=True)
    y = x * lax.rsqrt(ms + EPS) * g
    return y * (1.0 + scale) + shift


def _ada_kernel(c_ref, w_ref, b_ref, o_ref):
    c = c_ref[...]
    h = (c * jax.nn.sigmoid(c)).astype(BF16)
    o_ref[0] = _dot(h, w_ref[0].astype(BF16)) + b_ref[0]


def _ada_call(c_all, w_ada, b_ada):
    depth, d, n = w_ada.shape
    rows = c_all.shape[0]
    tn = 1152 if n % 1152 == 0 else n
    return pl.pallas_call(
        _ada_kernel,
        grid=(depth, n // tn),
        in_specs=[
            pl.BlockSpec((rows, d), lambda l, j: (0, 0)),
            pl.BlockSpec((1, d, tn), lambda l, j: (l, 0, j)),
            pl.BlockSpec((1, 1, tn), lambda l, j: (l, 0, j)),
        ],
        out_specs=pl.BlockSpec((1, rows, tn), lambda l, j: (l, 0, j)),
        out_shape=jax.ShapeDtypeStruct((depth, rows, n), F32),
        compiler_params=_params("arbitrary", "arbitrary"),
        name="ada",
    )(c_all, w_ada, b_ada.reshape(depth, 1, n))


def _mod_spec(mod, tiles_per_group):
    _, three, rows, d = mod.shape
    return pl.BlockSpec((1, three, rows, d), lambda i, *_: (i // tiles_per_group, 0, 0, 0))


def _tile(n, pref):
    return pref if n % pref == 0 else n


def _ffn_kernel(x_ref, g_ref, mod_ref, wg_ref, wu_ref, wo_ref, o_ref, h_scr, acc_scr):
    j = pl.program_id(1)

    @pl.when(j == 0)
    def _():
        h_scr[...] = _mod_norm(x_ref[...], g_ref[...], mod_ref[0, 1], mod_ref[0, 0]).astype(BF16)
        acc_scr[...] = jnp.zeros_like(acc_scr)

    h = h_scr[...]
    a = _dot(h, wg_ref[...])
    u = _dot(h, wu_ref[...])
    act = (a * jax.nn.sigmoid(a) * u).astype(BF16)
    acc_scr[...] += _dot(act, wo_ref[...])

    @pl.when(j == pl.num_programs(1) - 1)
    def _():
        o_ref[...] = x_ref[...] + (0.5 * mod_ref[0, 2]) * acc_scr[...]


def _ffn_call(x, g, mod, w_in, w_out, tiles_per_group, tm):
    n, d = x.shape
    f = w_out.shape[0]
    tf = f // 2 if (f // 2) % LANES == 0 else f
    nf = f // tf
    return pl.pallas_call(
        _ffn_kernel,
        grid=(n // tm, nf),
        in_specs=[
            pl.BlockSpec((tm, d), lambda i, j: (i, 0)),
            pl.BlockSpec((1, d), lambda i, j: (0, 0)),
            _mod_spec(mod, tiles_per_group),
            pl.BlockSpec((d, tf), lambda i, j: (0, j)),
            pl.BlockSpec((d, tf), lambda i, j: (0, nf + j)),
            pl.BlockSpec((tf, d), lambda i, j: (j, 0)),
        ],
        out_specs=pl.BlockSpec((tm, d), lambda i, j: (i, 0)),
        out_shape=jax.ShapeDtypeStruct((n, d), F32),
        scratch_shapes=[pltpu.VMEM((tm, d), BF16), pltpu.VMEM((tm, d), F32)],
        compiler_params=_params("arbitrary", "arbitrary"),
        name="ffn",
    )(x, g, mod, w_in, w_in, w_out)


def _s5_discretize(a_re, a_im, log_dt, b_re, b_im):
    dt = jnp.exp(log_dt)[:, None]
    mag = jnp.exp(a_re * dt)
    ab_re, ab_im = mag * jnp.cos(a_im * dt), mag * jnp.sin(a_im * dt)
    den = a_re * a_re + a_im * a_im
    n_re, n_im = ab_re - 1.0, ab_im
    w_re = (n_re * a_re + n_im * a_im) / den
    w_im = (n_im * a_re - n_re * a_im) / den
    bb_re = w_re[..., None] * b_re - w_im[..., None] * b_im
    bb_im = w_re[..., None] * b_im + w_im[..., None] * b_re
    return ab_re, ab_im, bb_re, bb_im


def _s5_weights(a_re, a_im, log_dt, b_re, b_im, c_re, c_im, d_skip):
    groups, state = a_re.shape
    ab_re, ab_im, bb_re, bb_im = _s5_discretize(a_re, a_im, log_dt, b_re, b_im)
    nkt = groups // 16
    eye16 = jnp.eye(16, dtype=F32)

    def bd(bb):
        t = bb.transpose(0, 2, 1).reshape(nkt, 16, SSM_GROUP, state)
        full = t[:, :, :, None, :] * eye16[None, :, None, :, None]
        return full.reshape(nkt, 16 * SSM_GROUP, 16 * state).astype(BF16)

    nj = groups // 8
    sel = (jnp.arange(16)[None, None, :] ==
           (jnp.arange(8)[None, :, None] + 8 * (jnp.arange(nj)[:, None, None] % 2))).astype(F32)

    def cd(cc):
        t = cc.transpose(0, 2, 1).reshape(nj, 8, state, SSM_GROUP)
        full = t[:, :, :, None, :] * sel[:, :, None, :, None]
        return full.reshape(nj, 8 * state, 16 * SSM_GROUP).astype(BF16)

    return dict(bd_re=bd(bb_re), bd_im=bd(bb_im), cd_re=cd(c_re), cd_im=cd(c_im),
                a_re=ab_re.reshape(1, groups * state), a_im=ab_im.reshape(1, groups * state),
                dd=d_skip.reshape(1, -1))


def _s5_tail(x, u, y_parts, dd, gate, wglu):
    y = jnp.concatenate(y_parts, axis=1) + dd * u
    z = _dot(jax.nn.gelu(y).astype(BF16), wglu)
    half = z.shape[1] // 2
    return x + gate * (z[:, :half] * jax.nn.sigmoid(z[:, half:]))


def _s5_prompt_kernel(x_ref, g_ref, mod_ref, win_ref, bdre_ref, bdim_ref, are_ref, aim_ref, cdre_ref, cdim_ref,
                      dd_ref, wglu_ref, o_ref, hre_ref, him_ref, sre_scr, sim_scr, cre_scr, cim_scr, *, t_len, pitch):
    @pl.when(pl.program_id(1) == 0)
    def _():
        cre_scr[...] = jnp.zeros_like(cre_scr)
        cim_scr[...] = jnp.zeros_like(cim_scr)

    x = x_ref[...]
    h = _mod_norm(x, g_ref[...], mod_ref[0, 1], mod_ref[0, 0]).astype(BF16)
    u = _dot(h, win_ref[...])
    ub = u.astype(BF16)
    nkt, kw, sw = bdre_ref.shape
    per_kt = sw // LANES
    nvreg = nkt * per_kt // SUBLANES
    for kt in range(nkt):
        ukt = ub[:, kt * kw:(kt + 1) * kw]
        bre = _dot(ukt, bdre_ref[kt])
        bim = _dot(ukt, bdim_ref[kt])
        for m in range(per_kt):
            r0 = (kt * per_kt + m) * pitch
            sre_scr[r0:r0 + t_len, :] = bre[:, m * LANES:(m + 1) * LANES]
            sim_scr[r0:r0 + t_len, :] = bim[:, m * LANES:(m + 1) * LANES]

    vrows = lambda k: slice(k * SUBLANES, (k + 1) * SUBLANES)
    a_re = [are_ref[vrows(k), :] for k in range(nvreg)]
    a_im = [aim_ref[vrows(k), :] for k in range(nvreg)]

    def step(t, carry):
        new = []
        for k in range(nvreg):
            sr, si = carry[2 * k], carry[2 * k + 1]
            rows = pl.ds(k * SUBLANES * pitch + t, SUBLANES, stride=pitch)
            nr = a_re[k] * sr - a_im[k] * si + sre_scr[rows, :]
            ni = a_re[k] * si + a_im[k] * sr + sim_scr[rows, :]
            sre_scr[rows, :] = nr
            sim_scr[rows, :] = ni
            new += [nr, ni]
        return tuple(new)

    init = []
    for k in range(nvreg):
        init += [cre_scr[vrows(k), :], cim_scr[vrows(k), :]]
    fin = lax.fori_loop(0, t_len, step, tuple(init), unroll=8)
    for k in range(nvreg):
        cre_scr[vrows(k), :] = fin[2 * k]
        cim_scr[vrows(k), :] = fin[2 * k + 1]
        hre_ref[0, vrows(k), :] = fin[2 * k]
        him_ref[0, vrows(k), :] = fin[2 * k + 1]

    y_parts = []
    for kt in range(nkt):
        slabs = lambda scr: jnp.concatenate(
            [scr[(kt * per_kt + m) * pitch:(kt * per_kt + m) * pitch + t_len, :] for m in range(per_kt)], axis=1)
        y_parts.append(_dot(slabs(sre_scr).astype(BF16), cdre_ref[kt]) - _dot(slabs(sim_scr).astype(BF16), cdim_ref[kt]))
    o_ref[...] = _s5_tail(x, u, y_parts, dd_ref[...], mod_ref[0, 2], wglu_ref[...])


def _s5_prompt_call(x, g, mod, w_in, sw, w_glu, bsz, t_len):
    n, d = x.shape
    seq = n // bsz
    nt = seq // t_len
    pitch = t_len + SUBLANES
    nstate = sw["a_re"].shape[1]
    nslab = nstate // LANES
    nkt = sw["bd_re"].shape[0]
    a_re = sw["a_re"].reshape(nslab, LANES)
    a_im = sw["a_im"].reshape(nslab, LANES)
    cd_re = sw["cd_re"].reshape(nkt, nstate // nkt, -1)
    cd_im = sw["cd_im"].reshape(nkt, nstate // nkt, -1)
    const = lambda *shape: pl.BlockSpec(shape, lambda b, t: (0,) * len(shape))
    out, hre, him = pl.pallas_call(
        functools.partial(_s5_prompt_kernel, t_len=t_len, pitch=pitch),
        grid=(bsz, nt),
        in_specs=[
            pl.BlockSpec((t_len, d), lambda b, t: (b * nt + t, 0)),
            const(1, d),
            pl.BlockSpec((1, 3, 1, d), lambda b, t: (b, 0, 0, 0)),
            const(*w_in.shape),
            const(*sw["bd_re"].shape), const(*sw["bd_im"].shape),
            const(nslab, LANES), const(nslab, LANES),
            const(*cd_re.shape), const(*cd_im.shape),
            const(1, d),
            const(*w_glu.shape),
        ],
        out_specs=[
            pl.BlockSpec((t_len, d), lambda b, t: (b * nt + t, 0)),
            pl.BlockSpec((1, nslab, LANES), lambda b, t: (b, 0, 0)),
            pl.BlockSpec((1, nslab, LANES), lambda b, t: (b, 0, 0)),
        ],
        out_shape=[jax.ShapeDtypeStruct((n, d), F32),
                   jax.ShapeDtypeStruct((bsz, nslab, LANES), F32),
                   jax.ShapeDtypeStruct((bsz, nslab, LANES), F32)],
        scratch_shapes=[pltpu.VMEM((nslab * pitch, LANES), F32), pltpu.VMEM((nslab * pitch, LANES), F32),
                        pltpu.VMEM((nslab, LANES), F32), pltpu.VMEM((nslab, LANES), F32)],
        compiler_params=_params("arbitrary", "arbitrary"),
        name="s5_prompt",
    )(x, g, mod, w_in, sw["bd_re"], sw["bd_im"], a_re, a_im, cd_re, cd_im, sw["dd"], w_glu)
    return out, hre, him


def _s5_sample_kernel(x_ref, g_ref, mod_ref, win_ref, bdre_ref, bdim_ref, are_ref, aim_ref, h0re_ref, h0im_ref,
                      cdre_ref, cdim_ref, dd_ref, wglu_ref, o_ref, hre_ref, him_ref, sre_scr, sim_scr):
    nt, bs, d = x_ref.shape
    x = x_ref[...].reshape(nt * bs, d)
    mods = [jnp.concatenate([mod_ref[0, k]] * nt, axis=0) for k in range(3)]
    h = _mod_norm(x, g_ref[...], mods[1], mods[0]).astype(BF16)
    u = _dot(h, win_ref[...])
    ub = u.astype(BF16)
    nkt = bdre_ref.shape[0]
    kw = bdre_ref.shape[1]
    sw = bdre_ref.shape[2]
    for kt in range(nkt):
        ukt = ub[:, kt * kw:(kt + 1) * kw]
        sre_scr[:, kt * sw:(kt + 1) * sw] = _dot(ukt, bdre_ref[kt])
        sim_scr[:, kt * sw:(kt + 1) * sw] = _dot(ukt, bdim_ref[kt])
    a_re = are_ref[...]
    a_im = aim_ref[...]
    sr = h0re_ref[...]
    si = h0im_ref[...]
    for t in range(nt):
        rows = slice(t * bs, (t + 1) * bs)
        nr = a_re * sr - a_im * si + sre_scr[rows, :]
        ni = a_re * si + a_im * sr + sim_scr[rows, :]
        sre_scr[rows, :] = nr
        sim_scr[rows, :] = ni
        sr, si = nr, ni
    hre_ref[...] = sr
    him_ref[...] = si
    y_parts = []
    for kt in range(nkt):
        cols = slice(kt * sw, (kt + 1) * sw)
        y_parts.append(_dot(sre_scr[:, cols].astype(BF16), cdre_ref[kt])
                       - _dot(sim_scr[:, cols].astype(BF16), cdim_ref[kt]))
    out = _s5_tail(x, u, y_parts, dd_ref[...], mods[2], wglu_ref[...])
    o_ref[...] = out.reshape(nt, bs, d)


def _s5_sample_call(x_tm, g, mod, w_in, sw, w_glu, h0_re, h0_im, bs):
    nt, nb, d = x_tm.shape
    nstate = sw["a_re"].shape[1]
    nkt = sw["bd_re"].shape[0]
    cd_re = sw["cd_re"].reshape(nkt, nstate // nkt, -1)
    cd_im = sw["cd_im"].reshape(nkt, nstate // nkt, -1)
    const = lambda *shape: pl.BlockSpec(shape, lambda i: (0,) * len(shape))
    return pl.pallas_call(
        _s5_sample_kernel,
        grid=(nb // bs,),
        in_specs=[
            pl.BlockSpec((nt, bs, d), lambda i: (0, i, 0)),
            const(1, d),
            pl.BlockSpec((1, 3, bs, d), lambda i: (i, 0, 0, 0)),
            const(*w_in.shape),
            const(*sw["bd_re"].shape), const(*sw["bd_im"].shape),
            const(1, nstate), const(1, nstate),
            pl.BlockSpec((bs, nstate), lambda i: (i, 0)),
            pl.BlockSpec((bs, nstate), lambda i: (i, 0)),
            const(*cd_re.shape), const(*cd_im.shape),
            const(1, d),
            const(*w_glu.shape),
        ],
        out_specs=[
            pl.BlockSpec((nt, bs, d), lambda i: (0, i, 0)),
            pl.BlockSpec((bs, nstate), lambda i: (i, 0)),
            pl.BlockSpec((bs, nstate), lambda i: (i, 0)),
        ],
        out_shape=[jax.ShapeDtypeStruct((nt, nb, d), F32),
                   jax.ShapeDtypeStruct((nb, nstate), F32),
                   jax.ShapeDtypeStruct((nb, nstate), F32)],
        scratch_shapes=[pltpu.VMEM((nt * bs, nstate), F32), pltpu.VMEM((nt * bs, nstate), F32)],
        compiler_params=_params("arbitrary"),
        name="s5_sample",
    )(x_tm, g, mod, w_in, sw["bd_re"], sw["bd_im"], sw["a_re"], sw["a_im"], h0_re, h0_im, cd_re, cd_im,
      sw["dd"], w_glu)


def _mla_weights(w_a, w_uq, w_ukv):
    d = w_a.shape[0]
    w_aq = w_a[:, :Q_LORA]
    w_akv = w_a[:, Q_LORA:Q_LORA + KV_LORA]
    w_akr = w_a[:, Q_LORA + KV_LORA:]
    kr1, kr2 = w_akr[:, :16], w_akr[:, 16:]
    z64, z32 = jnp.zeros((d, QK_NOPE), F32), jnp.zeros((d, 32), F32)
    wkr = jnp.concatenate([z64, kr1, kr2, z32], axis=1)
    wkr_sw = jnp.concatenate([z64, -kr2, kr1, z32], axis=1)
    wq = w_uq.reshape(Q_LORA, HEADS, QK_HEAD)
    qn, q1, q2 = wq[..., :QK_NOPE], wq[..., QK_NOPE:QK_NOPE + 16], wq[..., QK_NOPE + 16:]
    wuq = jnp.concatenate([qn, q1, q2, q2, q1], axis=-1).reshape(Q_LORA, HEADS * HEAD_PAD)
    wuq_sw = jnp.concatenate([jnp.zeros_like(qn), -q2, q1, q1, -q2], axis=-1).reshape(Q_LORA, HEADS * HEAD_PAD)
    wkv = w_ukv.reshape(KV_LORA, HEADS, QK_NOPE + V_HEAD)
    wk, wv = wkv[..., :QK_NOPE], wkv[..., QK_NOPE:]
    wuk = jnp.concatenate([wk, jnp.zeros_like(wk)], axis=-1).reshape(KV_LORA, HEADS * HEAD_PAD)
    wuv = wv.reshape(KV_LORA, HEADS * V_HEAD)
    wuk_t = wk.transpose(1, 2, 0)
    wabs = jnp.concatenate([wuk_t, jnp.zeros_like(wuk_t)], axis=1)
    eye_h = jnp.eye(HEADS, dtype=F32)
    wuv_bd = (wv.transpose(1, 0, 2)[:, :, None, :] * eye_h[:, None, :, None]).reshape(HEADS * KV_LORA, HEADS * V_HEAD)
    c = lambda w: w.astype(BF16)
    return dict(w_aq=c(w_aq), w_akv=c(w_akv), w_akr=c(w_akr), wkr=c(wkr), wkr_sw=c(wkr_sw), wuq=c(wuq),
                wuq_sw=c(wuq_sw), wuk=c(wuk), wuv=c(wuv), wuk_t=c(wuk_t.reshape(HEADS * QK_NOPE, KV_LORA)),
                wabs=c(wabs), wuv_bd=c(wuv_bd))


def _rope_cos_sin(pos):
    inv = 1.0 / (ROPE_THETA ** (jnp.arange(0, QK_ROPE, 2, dtype=F32) / QK_ROPE))
    ang = pos.astype(F32)[:, None] * inv[None, :]
    return jnp.cos(ang), jnp.sin(ang)


def _rope_tables(pos, g, scale, with_partner):
    c, s = _rope_cos_sin(pos)
    n = pos.shape[0]
    gn, g1, g2 = g[:QK_NOPE], g[QK_NOPE:QK_NOPE + 16], g[QK_NOPE + 16:]
    ones = jnp.ones((n, 1), F32)
    z16, z64 = jnp.zeros((n, 16), F32), jnp.zeros((n, QK_NOPE), F32)
    if with_partner:
        gc = jnp.concatenate([ones * gn, c * g1, c * g2, c * g2, -(c * g1)], axis=1)
        gs = jnp.concatenate([z64, s * g2, s * g1, s * g1, -(s * g2)], axis=1)
    else:
        gc = jnp.concatenate([ones * gn, c * g1, c * g2, z16, z16], axis=1)
        gs = jnp.concatenate([z64, s * g2, s * g1, z16, z16], axis=1)
    return gc * scale, gs * scale


def _mla_proj_kernel(x_ref, g_ref, mod_ref, waq_ref, wakv_ref, wakr_ref, wkr_ref, wkrsw_ref, gqa_ref, gkva_ref,
                     wuq_ref, wuqsw_ref, wuk_ref, wuv_ref, gcq_ref, gsq_ref, gck_ref, gsk_ref,
                     q_ref, k_ref, v_ref, ckv_ref, kr_ref, *, with_kv):
    x = x_ref[...]
    h = _mod_norm(x, g_ref[...], mod_ref[0, 1], mod_ref[0, 0]).astype(BF16)
    aq = _dot(h, waq_ref[...])
    akv = _dot(h, wakv_ref[...])
    kr_ref[...] = _dot(h, wakr_ref[...])
    cq = (aq * lax.rsqrt(jnp.mean(aq * aq, axis=-1, keepdims=True) + EPS) * gqa_ref[...]).astype(BF16)
    ckv = akv * lax.rsqrt(jnp.mean(akv * akv, axis=-1, keepdims=True) + EPS) * gkva_ref[...]
    ckv_ref[...] = ckv
    q_pre = _dot(cq, wuq_ref[...])
    q_sw = _dot(cq, wuqsw_ref[...])
    tm = x.shape[0]
    in_head = lax.broadcasted_iota(jnp.int32, (tm, HEAD_PAD), 1) < QK_HEAD
    gcq, gsq = gcq_ref[...], gsq_ref[...]
    for hd in range(HEADS):
        cols = slice(hd * HEAD_PAD, (hd + 1) * HEAD_PAD)
        qp = q_pre[:, cols]
        ssq = jnp.sum(jnp.where(in_head, qp * qp, 0.0), axis=-1, keepdims=True)
        rinv = lax.rsqrt(ssq * (1.0 / QK_HEAD) + EPS)
        q_ref[:, cols] = (rinv * (qp * gcq + q_sw[:, cols] * gsq)).astype(q_ref.dtype)
    if with_kv:
        ckv_b = ckv.astype(BF16)
        k_pre = _dot(ckv_b, wuk_ref[...])
        v_ref[...] = _dot(ckv_b, wuv_ref[...]).astype(v_ref.dtype)
        kr_slab = _dot(h, wkr_ref[...])
        kr_sw = _dot(h, wkrsw_ref[...])
        gck = gck_ref[...]
        kr_rot = kr_sw * gsk_ref[...]
        for hd in range(HEADS):
            cols = slice(hd * HEAD_PAD, (hd + 1) * HEAD_PAD)
            kp = k_pre[:, cols] + kr_slab
            ssq = jnp.sum(kp * kp, axis=-1, keepdims=True)
            rinv = lax.rsqrt(ssq * (1.0 / QK_HEAD) + EPS)
            k_ref[:, cols] = (rinv * (kp * gck + kr_rot)).astype(k_ref.dtype)
    else:
        k_ref[...] = jnp.zeros_like(k_ref)
        v_ref[...] = jnp.zeros_like(v_ref)


def _mla_proj_call(x, g, mod, mw, g_qa, g_kva, tabs, tiles_per_group, tm, with_kv, q_dtype):
    n, d = x.shape
    gcq, gsq, gck, gsk = tabs
    ntab = gcq.shape[0] // tm
    const = lambda a: pl.BlockSpec(a.shape, lambda i: (0,) * a.ndim)
    tab = pl.BlockSpec((tm, HEAD_PAD), lambda i: (i % ntab, 0))
    row = lambda w: pl.BlockSpec((tm, w), lambda i: (i, 0))
    kv_rows = tm if with_kv else SUBLANES
    kv_n = n if with_kv else SUBLANES * (n // tm)
    kvrow = lambda w: pl.BlockSpec((kv_rows, w), lambda i: (i, 0))
    g_qa2, g_kva2 = g_qa.reshape(1, -1), g_kva.reshape(1, -1)
    ws = [mw["w_aq"], mw["w_akv"], mw["w_akr"], mw["wkr"], mw["wkr_sw"], g_qa2, g_kva2,
          mw["wuq"], mw["wuq_sw"], mw["wuk"], mw["wuv"]]
    return pl.pallas_call(
        functools.partial(_mla_proj_kernel, with_kv=with_kv),
        grid=(n // tm,),
        in_specs=[row(d), pl.BlockSpec((1, d), lambda i: (0, 0)), _mod_spec(mod, tiles_per_group)]
                 + [const(w) for w in ws] + [tab, tab, tab, tab],
        out_specs=[row(HEADS * HEAD_PAD), kvrow(HEADS * HEAD_PAD), kvrow(HEADS * V_HEAD), row(KV_LORA),
                   row(QK_ROPE)],
        out_shape=[jax.ShapeDtypeStruct((n, HEADS * HEAD_PAD), q_dtype),
                   jax.ShapeDtypeStruct((kv_n, HEADS * HEAD_PAD), BF16),
                   jax.ShapeDtypeStruct((kv_n, HEADS * V_HEAD), BF16),
                   jax.ShapeDtypeStruct((n, KV_LORA), F32),
                   jax.ShapeDtypeStruct((n, QK_ROPE), F32)],
        compiler_params=_params("arbitrary"),
        name="mla_proj",
    )(x, g, mod, *ws, gcq, gsq, gck, gsk)


def _flash_kernel(qi_ref, ki_ref, q_ref, k_ref, v_ref, o_ref, m_scr, l_scr, acc_scr):
    p_id = pl.program_id(2)
    qi, ki = qi_ref[p_id], ki_ref[p_id]
    tq, tk = q_ref.shape[0], k_ref.shape[0]

    @pl.when(ki == 0)
    def _():
        m_scr[...] = jnp.full_like(m_scr, NEG_INF)
        l_scr[...] = jnp.zeros_like(l_scr)
        acc_scr[...] = jnp.zeros_like(acc_scr)

    rb = min(tq, FLASH_ROWS)

    def update(masked):
        units = [(i, hh) for i in range(tq // rb) for hh in range(2)]

        def n_keys(i):
            return (i + 1) * rb if masked else tk

        def scores(i, hh):
            cols = slice(hh * HEAD_PAD, (hh + 1) * HEAD_PAD)
            return _dot_nt(q_ref[i * rb:(i + 1) * rb, cols], k_ref[0:n_keys(i), cols])

        def finish(i, hh, s):
            rows = slice(i * rb, (i + 1) * rb)
            nk = n_keys(i)
            if masked:
                visible = (lax.broadcasted_iota(jnp.int32, (rb, nk), 1)
                           <= i * rb + lax.broadcasted_iota(jnp.int32, (rb, nk), 0))
                s = jnp.where(visible, s, NEG_INF)
            m_prev = m_scr[hh, rows, :]
            m_new = jnp.maximum(m_prev, jnp.max(s, axis=1, keepdims=True))
            alpha = jnp.exp2(m_prev - m_new)
            p = jnp.exp2(s - m_new)
            l_scr[hh, rows, :] = alpha * l_scr[hh, rows, :] + jnp.sum(p, axis=1, keepdims=True)
            acc_scr[hh, rows, :] = alpha * acc_scr[hh, rows, :] + _dot(p.astype(BF16), v_ref[0:nk, :])
            m_scr[hh, rows, :] = m_new

        s_next = scores(*units[0])
        for idx, unit in enumerate(units):
            s_cur = s_next
            if idx + 1 < len(units):
                s_next = scores(*units[idx + 1])
            finish(*unit, s_cur)

    @pl.when(ki < qi)
    def _():
        update(False)

    @pl.when(ki == qi)
    def _():
        update(True)

    @pl.when(ki == qi)
    def _():
        lane = lax.broadcasted_iota(jnp.int32, (tq, 2 * V_HEAD), 1)
        o_ref[...] = jnp.where(lane < V_HEAD, acc_scr[0] / l_scr[0], acc_scr[1] / l_scr[1]).astype(o_ref.dtype)


def _flash_call(q, k, v, bsz, tq):
    n = q.shape[0]
    seq = n // bsz
    nq = seq // tq
    pairs = [(a, b) for a in range(nq) for b in range(a + 1)]
    qi_tab = jnp.asarray([p[0] for p in pairs], jnp.int32)
    ki_tab = jnp.asarray([p[1] for p in pairs], jnp.int32)
    grid_spec = pltpu.PrefetchScalarGridSpec(
        num_scalar_prefetch=2,
        grid=(bsz, HEADS // 2, len(pairs)),
        in_specs=[
            pl.BlockSpec((tq, 2 * HEAD_PAD), lambda b, hp, p, qi, ki: (b * nq + qi[p], hp)),
            pl.BlockSpec((tq, 2 * HEAD_PAD), lambda b, hp, p, qi, ki: (b * nq + ki[p], hp)),
            pl.BlockSpec((tq, 2 * V_HEAD), lambda b, hp, p, qi, ki: (b * nq + ki[p], hp)),
        ],
        out_specs=pl.BlockSpec((tq, 2 * V_HEAD), lambda b, hp, p, qi, ki: (b * nq + qi[p], hp)),
        scratch_shapes=[pltpu.VMEM((2, tq, 1), F32), pltpu.VMEM((2, tq, 1), F32),
                        pltpu.VMEM((2, tq, 2 * V_HEAD), F32)],
    )
    return pl.pallas_call(
        _flash_kernel,
        grid_spec=grid_spec,
        out_shape=jax.ShapeDtypeStruct((n, HEADS * V_HEAD), BF16),
        compiler_params=_params("arbitrary", "arbitrary", "arbitrary"),
        name="flash_prompt",
    )(qi_tab, ki_tab, q, k, v)


def _absorb_kernel(q_ref, gk_ref, wabs_ref, o_ref):
    gk = gk_ref[...]
    for hd in range(HEADS):
        qn = (q_ref[:, hd * HEAD_PAD:(hd + 1) * HEAD_PAD] * gk).astype(BF16)
        o_ref[:, hd * KV_LORA:(hd + 1) * KV_LORA] = _dot(qn, wabs_ref[hd]).astype(o_ref.dtype)


def _absorb_call(q, gk_slab, wabs):
    n = q.shape[0]
    return pl.pallas_call(
        _absorb_kernel,
        grid=(1,),
        in_specs=[pl.BlockSpec(q.shape, lambda i: (0, 0)), pl.BlockSpec(gk_slab.shape, lambda i: (0, 0)),
                  pl.BlockSpec(wabs.shape, lambda i: (0, 0, 0))],
        out_specs=pl.BlockSpec((n, HEADS * KV_LORA), lambda i: (0, 0)),
        out_shape=jax.ShapeDtypeStruct((n, HEADS * KV_LORA), BF16),
        compiler_params=_params("arbitrary"),
        name="absorb_q",
    )(q, gk_slab, wabs)


def _chunk_scores(lhs, qx, latb, krt, gt):
    kq = _dot_nt(lhs, latb)
    half = krt.shape[0]
    xt = jnp.concatenate([krt * gt[:half], krt * gt[half:]], axis=0).astype(BF16)
    kx = _dot(qx, xt)
    return kq, kx, jnp.sum(krt * krt, axis=0, keepdims=True)


def _chunk_update(kq, kx, ssq_rope, latb, mask, m_scr, l_scr, acc_scr):
    nk = HEADS * QK_NOPE
    n = latb.shape[0]
    kn = kq[:nk]
    ssq = jnp.sum((kn * kn).reshape(HEADS, QK_NOPE, n), axis=1) + ssq_rope
    rinv = lax.rsqrt(ssq * (1.0 / QK_HEAD) + EPS)
    nrow = kx.shape[0]
    s = (kq[nk:] + kx) * jnp.concatenate([rinv] * (nrow // HEADS), axis=0)
    if mask is not None:
        s = jnp.where(mask, s, NEG_INF)
    m_prev = m_scr[...]
    m_new = jnp.maximum(m_prev, jnp.max(s, axis=1, keepdims=True))
    alpha = jnp.exp2(m_prev - m_new)
    p = jnp.exp2(s - m_new)
    l_scr[...] = alpha * l_scr[...] + jnp.sum(p, axis=1, keepdims=True)
    acc_scr[...] = alpha * acc_scr[...] + _dot(p.astype(BF16), latb)
    m_scr[...] = m_new


def _sample_attn_kernel(pt_ref, qa_ref, qx_ref, wukt_ref, *rest, pages, chunk, n_new):
    lat_refs = rest[:pages]
    kr_refs = rest[pages:2 * pages]
    gt_ref, latn_ref, krn_ref, gtn_ref, o_ref, lhs_scr, lat_scr, kr_scr, m_scr, l_scr, acc_scr = rest[2 * pages:]
    j = pl.program_id(1)
    nk = wukt_ref.shape[0]
    nrow = qa_ref.shape[0]

    @pl.when(j == 0)
    def _():
        lhs_scr[0:nk, :] = wukt_ref[...]
        lhs_scr[nk:nk + nrow, :] = qa_ref[...]
        m_scr[...] = jnp.full_like(m_scr, NEG_INF)
        l_scr[...] = jnp.zeros_like(l_scr)
        acc_scr[...] = jnp.zeros_like(acc_scr)

    for k in range(pages):
        lat_scr[k * PAGE:(k + 1) * PAGE, :] = lat_refs[k][...].astype(BF16)
        kr_scr[:, k * PAGE:(k + 1) * PAGE] = kr_refs[k][...]
    lhs = lhs_scr[...]
    qx = qx_ref[...]

    def stage_a(c):
        keys = slice(c * chunk, (c + 1) * chunk)
        return _chunk_scores(lhs, qx, lat_scr[keys, :], kr_scr[:, keys], gt_ref[:, keys])

    n_chunks = pages * PAGE // chunk
    nxt = stage_a(0)
    for c in range(n_chunks):
        cur = nxt
        if c + 1 < n_chunks:
            nxt = stage_a(c + 1)
        _chunk_update(*cur, lat_scr[c * chunk:(c + 1) * chunk, :], None, m_scr, l_scr, acc_scr)

    @pl.when(j == pl.num_programs(1) - 1)
    def _():
        t_row = lax.broadcasted_iota(jnp.int32, (nrow, PAGE), 0) // HEADS
        kk = lax.broadcasted_iota(jnp.int32, (nrow, PAGE), 1)
        mask = (kk < n_new) & (kk <= t_row)
        latb = latn_ref[...].astype(BF16)
        _chunk_update(*_chunk_scores(lhs, qx, latb, krn_ref[...], gtn_ref[...]), latb, mask, m_scr, l_scr, acc_scr)
        o_ref[...] = acc_scr[...] / l_scr[...]


def _sample_attn_call(page_table, qabs, qx, wuk_t, cache_latent, cache_krope_t, layer, gt, lat_new, krt_new, gt_new,
                      n_new, pages):
    nb, n_pages = page_table.shape
    nj = n_pages // pages
    nrow = qabs.shape[1]
    chunk = min(pages * PAGE, SAMPLE_CHUNK)
    pt_flat = page_table.reshape(-1)

    def page_spec(shape, k):
        return pl.BlockSpec((None, None) + shape,
                            lambda b, j, pt: (layer, pt[b * n_pages + j * pages + k], 0, 0))

    grid_spec = pltpu.PrefetchScalarGridSpec(
        num_scalar_prefetch=1,
        grid=(nb, nj),
        in_specs=[
            pl.BlockSpec((None, nrow, KV_LORA), lambda b, j, pt: (b, 0, 0)),
            pl.BlockSpec((None,) + qx.shape[1:], lambda b, j, pt: (b, 0, 0)),
            pl.BlockSpec(wuk_t.shape, lambda b, j, pt: (0, 0)),
        ] + [page_spec((PAGE, KV_LORA), k) for k in range(pages)]
          + [page_spec((QK_ROPE, PAGE), k) for k in range(pages)] + [
            pl.BlockSpec((2 * QK_ROPE, pages * PAGE), lambda b, j, pt: (0, j)),
            pl.BlockSpec((None, PAGE, KV_LORA), lambda b, j, pt: (b, 0, 0)),
            pl.BlockSpec((None, QK_ROPE, PAGE), lambda b, j, pt: (b, 0, 0)),
            pl.BlockSpec((2 * QK_ROPE, PAGE), lambda b, j, pt: (0, 0)),
        ],
        out_specs=pl.BlockSpec((None, nrow, KV_LORA), lambda b, j, pt: (b, 0, 0)),
        scratch_shapes=[pltpu.VMEM((wuk_t.shape[0] + nrow, KV_LORA), BF16),
                        pltpu.VMEM((pages * PAGE, KV_LORA), BF16),
                        pltpu.VMEM((QK_ROPE, pages * PAGE), F32),
                        pltpu.VMEM((nrow, 1), F32), pltpu.VMEM((nrow, 1), F32),
                        pltpu.VMEM((nrow, KV_LORA), F32)],
    )
    return pl.pallas_call(
        functools.partial(_sample_attn_kernel, pages=pages, chunk=chunk, n_new=n_new),
        grid_spec=grid_spec,
        out_shape=jax.ShapeDtypeStruct((nb, nrow, KV_LORA), F32),
        compiler_params=_params("arbitrary", "arbitrary"),
        name="sample_attn",
    )(pt_flat, qabs, qx, wuk_t, *([cache_latent] * pages), *([cache_krope_t] * pages), gt, lat_new, krt_new, gt_new)


def _attn_out_kernel(x_ref, a_ref, mod_ref, *rest, pre_proj):
    if pre_proj:
        wuv_ref, wo_ref, o_ref = rest
        a = _dot(a_ref[...].astype(BF16), wuv_ref[...]).astype(BF16)
    else:
        wo_ref, o_ref = rest
        a = a_ref[...].astype(BF16)
    o_ref[...] = x_ref[...] + mod_ref[0, 2] * _dot(a, wo_ref[...])


def _attn_out_call(x, a, mod, w_o, tiles_per_group, tm, wuv_bd=None):
    n, d = x.shape
    ws = [w_o] if wuv_bd is None else [wuv_bd, w_o]
    return pl.pallas_call(
        functools.partial(_attn_out_kernel, pre_proj=wuv_bd is not None),
        grid=(n // tm,),
        in_specs=[pl.BlockSpec((tm, d), lambda i: (i, 0)), pl.BlockSpec((tm, a.shape[1]), lambda i: (i, 0)),
                  _mod_spec(mod, tiles_per_group)] + [pl.BlockSpec(w.shape, lambda i: (0, 0)) for w in ws],
        out_specs=pl.BlockSpec((tm, d), lambda i: (i, 0)),
        out_shape=jax.ShapeDtypeStruct((n, d), F32),
        compiler_params=_params("arbitrary"),
        name="attn_out",
    )(x, a, mod, *ws)


def kernel(x_prompt, x_sample, c_prompt, c_sample, cache_latent, cache_krope, state_ssm_re, state_ssm_im, page_table, w_ada, b_ada, norm_g, ffn_w_in, ffn_w_out, ssm_w_in, ssm_a_re, ssm_a_im, ssm_log_dt, ssm_b_re, ssm_b_im, ssm_c_re, ssm_c_im, ssm_d, ssm_w_glu, mla_w_a, mla_g_qa, mla_g_kva, mla_w_uq, mla_w_ukv, mla_g_q, mla_g_k, mla_w_o):
    bsz, seq, d = x_prompt.shape
    nb, nt, _ = x_sample.shape
    depth = w_ada.shape[0]
    n_past = page_table.shape[1] * PAGE

    tm_p = _tile(seq, TOKEN_TILE)
    tpg_p = seq // tm_p
    n_s = nb * nt
    s5_t = _tile(seq, S5_CHUNK)
    s5_bs = _tile(nb, S5_SEQS)
    tq = _tile(seq, FLASH_TILE)
    pages = _tile(page_table.shape[1], PAGES_PER_STEP)

    c_all = jnp.concatenate([c_prompt, c_sample], axis=0)
    pad = (-c_all.shape[0]) % SUBLANES
    ada = _ada_call(jnp.pad(c_all, ((0, pad), (0, 0))), w_ada, b_ada)
    ada = ada.reshape(depth, -1, N_ADA, d)

    xp = x_prompt.reshape(bsz * seq, d)
    xs = x_sample.transpose(1, 0, 2).reshape(n_s, d)

    pos_p = jnp.arange(seq, dtype=jnp.int32)
    pos_s = n_past + jnp.arange(nt, dtype=jnp.int32)
    outs = {k: [] for k in ("lat_p", "kr_p", "lat_s", "kr_s", "hre_p", "him_p", "hre_s", "him_s")}

    for i in range(depth):
        ada_p = ada[i, :bsz]
        ada_s = ada[i, bsz:bsz + nb]
        mod_p = [ada_p[:, 3 * j:3 * j + 3, None, :] for j in range(3)]
        mod_s_seq = [ada_s[:, 3 * j:3 * j + 3].transpose(1, 0, 2) for j in range(3)]
        mod_s = [jnp.tile(m, (1, nt, 1))[None] for m in mod_s_seq]
        g = norm_g[i].reshape(3, 1, d)
        w_in = ffn_w_in[i].astype(BF16)
        w_out = ffn_w_out[i].astype(BF16)

        xp = _ffn_call(xp, g[0], mod_p[0], w_in[0], w_out[0], tpg_p, tm_p)
        xs = _ffn_call(xs, g[0], mod_s[0], w_in[0], w_out[0], 1, n_s)

        j = i // 2
        if i % 2 == 0:
            sw = _s5_weights(ssm_a_re[j], ssm_a_im[j], ssm_log_dt[j], ssm_b_re[j], ssm_b_im[j],
                             ssm_c_re[j], ssm_c_im[j], ssm_d[j])
            w_sin = ssm_w_in[j].astype(BF16)
            w_glu = ssm_w_glu[j].astype(BF16)
            xp, hre, him = _s5_prompt_call(xp, g[1], mod_p[1], w_sin, sw, w_glu, bsz, s5_t)
            groups, state = ssm_a_re[j].shape
            outs["hre_p"].append(hre.reshape(bsz, groups, state))
            outs["him_p"].append(him.reshape(bsz, groups, state))
            mod_blk = mod_s_seq[1].reshape(3, nb // s5_bs, s5_bs, d).transpose(1, 0, 2, 3)
            xs3, hre, him = _s5_sample_call(xs.reshape(nt, nb, d), g[1], mod_blk, w_sin, sw, w_glu,
                                            state_ssm_re[j].reshape(nb, -1), state_ssm_im[j].reshape(nb, -1), s5_bs)
            xs = xs3.reshape(n_s, d)
            outs["hre_s"].append(hre.reshape(nb, groups, state))
            outs["him_s"].append(him.reshape(nb, groups, state))
        else:
            mw = _mla_weights(mla_w_a[j], mla_w_uq[j], mla_w_ukv[j])
            w_o = mla_w_o[j].astype(BF16)
            g_q, g_k = mla_g_q[j], mla_g_k[j]
            tabs_p = _rope_tables(pos_p, g_q, ATTN_SCALE * LOG2E, False) + _rope_tables(pos_p, g_k, 1.0, False)
            q, k, v, ckv, kr = _mla_proj_call(xp, g[1], mod_p[1], mw, mla_g_qa[j], mla_g_kva[j], tabs_p, tpg_p, tm_p,
                                              True, BF16)
            outs["lat_p"].append(ckv.reshape(bsz, seq, KV_LORA))
            outs["kr_p"].append(kr.reshape(bsz, seq, QK_ROPE))
            attn = _flash_call(q, k, v, bsz, tq)
            xp = _attn_out_call(xp, attn, mod_p[1], w_o, tpg_p, tm_p)
            pos_rows = jnp.repeat(pos_s, nb)
            tabs_s = _rope_tables(pos_rows, g_q, ATTN_SCALE * LOG2E, True) + _rope_tables(pos_rows, g_k, 1.0, False)
            q_s, _, _, ckv_s, kr_s = _mla_proj_call(xs, g[1], mod_s[1], mw, mla_g_qa[j], mla_g_kva[j], tabs_s, 1, n_s,
                                                    False, F32)
            lat_new = ckv_s.reshape(nt, nb, KV_LORA).transpose(1, 0, 2)
            kr_new = kr_s.reshape(nt, nb, QK_ROPE).transpose(1, 0, 2)
            outs["lat_s"].append(lat_new)
            outs["kr_s"].append(kr_new)
            q_sm = q_s.reshape(nt, nb, HEADS * HEAD_PAD).transpose(1, 0, 2).reshape(n_s, HEADS * HEAD_PAD)
            gk_slab = jnp.concatenate([g_k[:QK_NOPE], jnp.zeros((HEAD_PAD - QK_NOPE,), F32)]).reshape(1, HEAD_PAD)
            qabs = _absorb_call(q_sm, gk_slab, mw["wabs"]).reshape(nb, nt * HEADS, KV_LORA)
            qx = q_sm.reshape(nb, nt * HEADS, HEAD_PAD)[..., QK_NOPE:].astype(BF16)
            pos_k = jnp.arange(n_past + PAGE, dtype=jnp.int32)
            ck, sk = _rope_cos_sin(pos_k)
            g1, g2 = g_k[QK_NOPE:QK_NOPE + 16], g_k[QK_NOPE + 16:]
            gt = jnp.concatenate([ck * g1, ck * g2, sk * g1, sk * g2], axis=1).T
            padn = ((0, 0), (0, PAGE - nt), (0, 0))
            krt_new = jnp.pad(kr_new, padn).transpose(0, 2, 1)
            ctx = _sample_attn_call(page_table, qabs, qx, mw["wuk_t"], cache_latent, cache_krope.transpose(0, 1, 3, 2),
                                    j, gt[:, :n_past], jnp.pad(lat_new, padn), krt_new, gt[:, n_past:], nt, pages)
            ctx_tm = ctx.reshape(nb, nt, HEADS * KV_LORA).transpose(1, 0, 2).reshape(n_s, HEADS * KV_LORA)
            xs = _attn_out_call(xs, ctx_tm, mod_s[1], w_o, 1, n_s, wuv_bd=mw["wuv_bd"])

        xp = _ffn_call(xp, g[2], mod_p[2], w_in[1], w_out[1], tpg_p, tm_p)
        xs = _ffn_call(xs, g[2], mod_s[2], w_in[1], w_out[1], 1, n_s)

    y_prompt = xp.reshape(bsz, seq, d)
    y_sample = xs.reshape(nt, nb, d).transpose(1, 0, 2)
    st = lambda k: jnp.stack(outs[k])
    return (y_prompt, y_sample, st("lat_p"), st("kr_p"), st("lat_s"), st("kr_s"),
            st("hre_p"), st("him_p"), st("hre_s"), st("him_s"))
```

```python
import functools
import math

import jax
import jax.numpy as jnp
from jax import lax
from jax.experimental import pallas as pl
from jax.experimental.pallas import tpu as pltpu

F32 = jnp.float32
BF16 = jnp.bfloat16

EPS = 1e-6
NEG_INF = -1e30
N_ADA = 9
PAGE = 128
SSM_GROUP = 16
HEADS = 16
Q_LORA = 384
KV_LORA = 256
QK_NOPE = 64
QK_ROPE = 32
QK_HEAD = QK_NOPE + QK_ROPE
V_HEAD = 64
HEAD_PAD = 128
ROPE_THETA = 10000.0
ATTN_SCALE = 1.0 / math.sqrt(QK_HEAD)
LOG2E = math.log2(math.e)
DT_MIN = 1e-3
DT_MAX = 1e-1

LANES = 128
SUBLANES = 8
VMEM_LIMIT = 56 * 1024 * 1024

TOKEN_TILE = 512
S5_CHUNK = 256
S5_SEQS = 64
FLASH_TILE = 1024
FLASH_ROWS = 256
PAGES_PER_STEP = 32
SAMPLE_CHUNK = 512


def _params(*sem):
    return pltpu.CompilerParams(dimension_semantics=sem, vmem_limit_bytes=VMEM_LIMIT)


def _dot(a, b):
    return jnp.dot(a, b, preferred_element_type=F32)


def _dot_nt(a, b):
    return lax.dot_general(a, b, (((1,), (1,)), ((), ())), preferred_element_type=F32)


def _mod_norm(x, g, scale, shift):
    ms = jnp.mean(x * x, axis=-1, keepdims=True)
    y = x * lax.rsqrt(ms + EPS) * g
    return y * (1.0 + scale) + shift


def _ada_kernel(c_ref, w_ref, b_ref, o_ref):
    c = c_ref[...]
    h = (c * jax.nn.sigmoid(c)).astype(BF16)
    o_ref[0] = _dot(h, w_ref[0].astype(BF16)) + b_ref[0]


def _ada_call(c_all, w_ada, b_ada):
    depth, d, n = w_ada.shape
    rows = c_all.shape[0]
    tn = 1152 if n % 1152 == 0 else n
    return pl.pallas_call(
        _ada_kernel,
        grid=(depth, n // tn),
        in_specs=[
            pl.BlockSpec((rows, d), lambda l, j: (0, 0)),
            pl.BlockSpec((1, d, tn), lambda l, j: (l, 0, j)),
            pl.BlockSpec((1, 1, tn), lambda l, j: (l, 0, j)),
        ],
        out_specs=pl.BlockSpec((1, rows, tn), lambda l, j: (l, 0, j)),
        out_shape=jax.ShapeDtypeStruct((depth, rows, n), F32),
        compiler_params=_params("arbitrary", "arbitrary"),
        name="ada",
    )(c_all, w_ada, b_ada.reshape(depth, 1, n))


def _mod_spec(mod, tiles_per_group):
    _, three, rows, d = mod.shape
    return pl.BlockSpec((1, three, rows, d), lambda i, *_: (i // tiles_per_group, 0, 0, 0))


def _tile(n, pref):
    return pref if n % pref == 0 else n


def _ffn_kernel(x_ref, g_ref, mod_ref, wg_ref, wu_ref, wo_ref, o_ref, h_scr, acc_scr):
    j = pl.program_id(1)

    @pl.when(j == 0)
    def _():
        h_scr[...] = _mod_norm(x_ref[...], g_ref[...], mod_ref[0, 1], mod_ref[0, 0]).astype(BF16)
        acc_scr[...] = jnp.zeros_like(acc_scr)

    h = h_scr[...]
    a = _dot(h, wg_ref[...])
    u = _dot(h, wu_ref[...])
    act = (a * jax.nn.sigmoid(a) * u).astype(BF16)
    acc_scr[...] += _dot(act, wo_ref[...])

    @pl.when(j == pl.num_programs(1) - 1)
    def _():
        o_ref[...] = x_ref[...] + (0.5 * mod_ref[0, 2]) * acc_scr[...]


def _ffn_call(x, g, mod, w_in, w_out, which, tiles_per_group, tm):
    n, d = x.shape
    f = w_out.shape[2]
    tf = f // 2 if (f // 2) % LANES == 0 else f
    nf = f // tf
    layer, k = which
    return pl.pallas_call(
        _ffn_kernel,
        grid=(n // tm, nf),
        in_specs=[
            pl.BlockSpec((tm, d), lambda i, j: (i, 0)),
            pl.BlockSpec((1, d), lambda i, j: (0, 0)),
            _mod_spec(mod, tiles_per_group),
            pl.BlockSpec((None, None, d, tf), lambda i, j: (layer, k, 0, j)),
            pl.BlockSpec((None, None, d, tf), lambda i, j: (layer, k, 0, nf + j)),
            pl.BlockSpec((None, None, tf, d), lambda i, j: (layer, k, j, 0)),
        ],
        out_specs=pl.BlockSpec((tm, d), lambda i, j: (i, 0)),
        out_shape=jax.ShapeDtypeStruct((n, d), F32),
        scratch_shapes=[pltpu.VMEM((tm, d), BF16), pltpu.VMEM((tm, d), F32)],
        compiler_params=_params("arbitrary", "arbitrary"),
        name="ffn",
    )(x, g, mod, w_in, w_in, w_out)


def _s5_discretize(a_re, a_im, log_dt, b_re, b_im):
    dt = jnp.exp(log_dt)[:, None]
    mag = jnp.exp(a_re * dt)
    ab_re, ab_im = mag * jnp.cos(a_im * dt), mag * jnp.sin(a_im * dt)
    den = a_re * a_re + a_im * a_im
    n_re, n_im = ab_re - 1.0, ab_im
    w_re = (n_re * a_re + n_im * a_im) / den
    w_im = (n_im * a_re - n_re * a_im) / den
    bb_re = w_re[..., None] * b_re - w_im[..., None] * b_im
    bb_im = w_re[..., None] * b_im + w_im[..., None] * b_re
    return ab_re, ab_im, bb_re, bb_im


def _s5_weights(a_re, a_im, log_dt, b_re, b_im, c_re, c_im, d_skip):
    groups, state = a_re.shape
    ab_re, ab_im, bb_re, bb_im = _s5_discretize(a_re, a_im, log_dt, b_re, b_im)
    nkt = groups // 16
    eye16 = jnp.eye(16, dtype=F32)

    def bd(bb):
        t = bb.transpose(0, 2, 1).reshape(nkt, 16, SSM_GROUP, state)
        full = t[:, :, :, None, :] * eye16[None, :, None, :, None]
        return full.reshape(nkt, 16 * SSM_GROUP, 16 * state).astype(BF16)

    nj = groups // 8
    sel = (jnp.arange(16)[None, None, :] ==
           (jnp.arange(8)[None, :, None] + 8 * (jnp.arange(nj)[:, None, None] % 2))).astype(F32)

    def cd(cc):
        t = cc.transpose(0, 2, 1).reshape(nj, 8, state, SSM_GROUP)
        full = t[:, :, :, None, :] * sel[:, :, None, :, None]
        return full.reshape(nj, 8 * state, 16 * SSM_GROUP).astype(BF16)

    return dict(bd_re=bd(bb_re), bd_im=bd(bb_im), cd_re=cd(c_re), cd_im=cd(c_im),
                a_re=ab_re.reshape(1, groups * state), a_im=ab_im.reshape(1, groups * state),
                dd=d_skip.reshape(1, -1))


def _s5_tail(x, u, y_parts, dd, gate, wglu):
    y = jnp.concatenate(y_parts, axis=1) + dd * u
    z = _dot(jax.nn.gelu(y).astype(BF16), wglu)
    half = z.shape[1] // 2
    return x + gate * (z[:, :half] * jax.nn.sigmoid(z[:, half:]))


def _s5_prompt_kernel(x_ref, g_ref, mod_ref, win_ref, bdre_ref, bdim_ref, are_ref, aim_ref, cdre_ref, cdim_ref,
                      dd_ref, wglu_ref, o_ref, hre_ref, him_ref, sre_scr, sim_scr, cre_scr, cim_scr, *, t_len, pitch):
    @pl.when(pl.program_id(1) == 0)
    def _():
        cre_scr[...] = jnp.zeros_like(cre_scr)
        cim_scr[...] = jnp.zeros_like(cim_scr)

    x = x_ref[...]
    h = _mod_norm(x, g_ref[...], mod_ref[0, 1], mod_ref[0, 0]).astype(BF16)
    u = _dot(h, win_ref[...])
    ub = u.astype(BF16)
    nkt, kw, sw = bdre_ref.shape
    per_kt = sw // LANES
    nvreg = nkt * per_kt // SUBLANES
    for kt in range(nkt):
        ukt = ub[:, kt * kw:(kt + 1) * kw]
        bre = _dot(ukt, bdre_ref[kt])
        bim = _dot(ukt, bdim_ref[kt])
        for m in range(per_kt):
            r0 = (kt * per_kt + m) * pitch
            sre_scr[r0:r0 + t_len, :] = bre[:, m * LANES:(m + 1) * LANES]
            sim_scr[r0:r0 + t_len, :] = bim[:, m * LANES:(m + 1) * LANES]

    vrows = lambda k: slice(k * SUBLANES, (k + 1) * SUBLANES)
    a_re = [are_ref[vrows(k), :] for k in range(nvreg)]
    a_im = [aim_ref[vrows(k), :] for k in range(nvreg)]

    def step(t, carry):
        new = []
        for k in range(nvreg):
            sr, si = carry[2 * k], carry[2 * k + 1]
            rows = pl.ds(k * SUBLANES * pitch + t, SUBLANES, stride=pitch)
            nr = a_re[k] * sr - a_im[k] * si + sre_scr[rows, :]
            ni = a_re[k] * si + a_im[k] * sr + sim_scr[rows, :]
            sre_scr[rows, :] = nr
            sim_scr[rows, :] = ni
            new += [nr, ni]
        return tuple(new)

    init = []
    for k in range(nvreg):
        init += [cre_scr[vrows(k), :], cim_scr[vrows(k), :]]
    fin = lax.fori_loop(0, t_len, step, tuple(init), unroll=8)
    for k in range(nvreg):
        cre_scr[vrows(k), :] = fin[2 * k]
        cim_scr[vrows(k), :] = fin[2 * k + 1]
        hre_ref[0, vrows(k), :] = fin[2 * k]
        him_ref[0, vrows(k), :] = fin[2 * k + 1]

    y_parts = []
    for kt in range(nkt):
        slabs = lambda scr: jnp.concatenate(
            [scr[(kt * per_kt + m) * pitch:(kt * per_kt + m) * pitch + t_len, :] for m in range(per_kt)], axis=1)
        y_parts.append(_dot(slabs(sre_scr).astype(BF16), cdre_ref[kt]) - _dot(slabs(sim_scr).astype(BF16), cdim_ref[kt]))
    o_ref[...] = _s5_tail(x, u, y_parts, dd_ref[...], mod_ref[0, 2], wglu_ref[...])


def _s5_prompt_call(x, g, mod, w_in, sw, w_glu, bsz, t_len):
    n, d = x.shape
    seq = n // bsz
    nt = seq // t_len
    pitch = t_len + SUBLANES
    nstate = sw["a_re"].shape[1]
    nslab = nstate // LANES
    nkt = sw["bd_re"].shape[0]
    a_re = sw["a_re"].reshape(nslab, LANES)
    a_im = sw["a_im"].reshape(nslab, LANES)
    cd_re = sw["cd_re"].reshape(nkt, nstate // nkt, -1)
    cd_im = sw["cd_im"].reshape(nkt, nstate // nkt, -1)
    const = lambda *shape: pl.BlockSpec(shape, lambda b, t: (0,) * len(shape))
    out, hre, him = pl.pallas_call(
        functools.partial(_s5_prompt_kernel, t_len=t_len, pitch=pitch),
        grid=(bsz, nt),
        in_specs=[
            pl.BlockSpec((t_len, d), lambda b, t: (b * nt + t, 0)),
            const(1, d),
            pl.BlockSpec((1, 3, 1, d), lambda b, t: (b, 0, 0, 0)),
            const(*w_in.shape),
            const(*sw["bd_re"].shape), const(*sw["bd_im"].shape),
            const(nslab, LANES), const(nslab, LANES),
            const(*cd_re.shape), const(*cd_im.shape),
            const(1, d),
            const(*w_glu.shape),
        ],
        out_specs=[
            pl.BlockSpec((t_len, d), lambda b, t: (b * nt + t, 0)),
            pl.BlockSpec((1, nslab, LANES), lambda b, t: (b, 0, 0)),
            pl.BlockSpec((1, nslab, LANES), lambda b, t: (b, 0, 0)),
        ],
        out_shape=[jax.ShapeDtypeStruct((n, d), F32),
                   jax.ShapeDtypeStruct((bsz, nslab, LANES), F32),
                   jax.ShapeDtypeStruct((bsz, nslab, LANES), F32)],
        scratch_shapes=[pltpu.VMEM((nslab * pitch, LANES), F32), pltpu.VMEM((nslab * pitch, LANES), F32),
                        pltpu.VMEM((nslab, LANES), F32), pltpu.VMEM((nslab, LANES), F32)],
        compiler_params=_params("arbitrary", "arbitrary"),
        name="s5_prompt",
    )(x, g, mod, w_in, sw["bd_re"], sw["bd_im"], a_re, a_im, cd_re, cd_im, sw["dd"], w_glu)
    return out, hre, him


def _s5_sample_kernel(x_ref, g_ref, mod_ref, win_ref, bdre_ref, bdim_ref, are_ref, aim_ref, h0re_ref, h0im_ref,
                      cdre_ref, cdim_ref, dd_ref, wglu_ref, o_ref, hre_ref, him_ref, sre_scr, sim_scr):
    nt, bs, d = x_ref.shape
    x = x_ref[...].reshape(nt * bs, d)
    mods = [jnp.concatenate([mod_ref[0, k]] * nt, axis=0) for k in range(3)]
    h = _mod_norm(x, g_ref[...], mods[1], mods[0]).astype(BF16)
    u = _dot(h, win_ref[...])
    ub = u.astype(BF16)
    nkt = bdre_ref.shape[0]
    kw = bdre_ref.shape[1]
    sw = bdre_ref.shape[2]
    for kt in range(nkt):
        ukt = ub[:, kt * kw:(kt + 1) * kw]
        sre_scr[:, kt * sw:(kt + 1) * sw] = _dot(ukt, bdre_ref[kt])
        sim_scr[:, kt * sw:(kt + 1) * sw] = _dot(ukt, bdim_ref[kt])
    a_re = are_ref[...]
    a_im = aim_ref[...]
    sr = h0re_ref[...]
    si = h0im_ref[...]
    for t in range(nt):
        rows = slice(t * bs, (t + 1) * bs)
        nr = a_re * sr - a_im * si + sre_scr[rows, :]
        ni = a_re * si + a_im * sr + sim_scr[rows, :]
        sre_scr[rows, :] = nr
        sim_scr[rows, :] = ni
        sr, si = nr, ni
    hre_ref[...] = sr
    him_ref[...] = si
    y_parts = []
    for kt in range(nkt):
        cols = slice(kt * sw, (kt + 1) * sw)
        y_parts.append(_dot(sre_scr[:, cols].astype(BF16), cdre_ref[kt])
                       - _dot(sim_scr[:, cols].astype(BF16), cdim_ref[kt]))
    out = _s5_tail(x, u, y_parts, dd_ref[...], mods[2], wglu_ref[...])
    o_ref[...] = out.reshape(nt, bs, d)


def _s5_sample_call(x_tm, g, mod, w_in, sw, w_glu, h0_re, h0_im, bs):
    nt, nb, d = x_tm.shape
    nstate = sw["a_re"].shape[1]
    nkt = sw["bd_re"].shape[0]
    cd_re = sw["cd_re"].reshape(nkt, nstate // nkt, -1)
    cd_im = sw["cd_im"].reshape(nkt, nstate // nkt, -1)
    const = lambda *shape: pl.BlockSpec(shape, lambda i: (0,) * len(shape))
    return pl.pallas_call(
        _s5_sample_kernel,
        grid=(nb // bs,),
        in_specs=[
            pl.BlockSpec((nt, bs, d), lambda i: (0, i, 0)),
            const(1, d),
            pl.BlockSpec((1, 3, bs, d), lambda i: (i, 0, 0, 0)),
            const(*w_in.shape),
            const(*sw["bd_re"].shape), const(*sw["bd_im"].shape),
            const(1, nstate), const(1, nstate),
            pl.BlockSpec((bs, nstate), lambda i: (i, 0)),
            pl.BlockSpec((bs, nstate), lambda i: (i, 0)),
            const(*cd_re.shape), const(*cd_im.shape),
            const(1, d),
            const(*w_glu.shape),
        ],
        out_specs=[
            pl.BlockSpec((nt, bs, d), lambda i: (0, i, 0)),
            pl.BlockSpec((bs, nstate), lambda i: (i, 0)),
            pl.BlockSpec((bs, nstate), lambda i: (i, 0)),
        ],
        out_shape=[jax.ShapeDtypeStruct((nt, nb, d), F32),
                   jax.ShapeDtypeStruct((nb, nstate), F32),
                   jax.ShapeDtypeStruct((nb, nstate), F32)],
        scratch_shapes=[pltpu.VMEM((nt * bs, nstate), F32), pltpu.VMEM((nt * bs, nstate), F32)],
        compiler_params=_params("arbitrary"),
        name="s5_sample",
    )(x_tm, g, mod, w_in, sw["bd_re"], sw["bd_im"], sw["a_re"], sw["a_im"], h0_re, h0_im, cd_re, cd_im,
      sw["dd"], w_glu)


def _mla_weights(w_a, w_uq, w_ukv):
    d = w_a.shape[0]
    w_aq = w_a[:, :Q_LORA]
    w_akv = w_a[:, Q_LORA:Q_LORA + KV_LORA]
    w_akr = w_a[:, Q_LORA + KV_LORA:]
    kr1, kr2 = w_akr[:, :16], w_akr[:, 16:]
    z64, z32 = jnp.zeros((d, QK_NOPE), F32), jnp.zeros((d, 32), F32)
    wkr = jnp.concatenate([z64, kr1, kr2, z32], axis=1)
    wkr_sw = jnp.concatenate([z64, -kr2, kr1, z32], axis=1)
    wq = w_uq.reshape(Q_LORA, HEADS, QK_HEAD)
    qn, q1, q2 = wq[..., :QK_NOPE], wq[..., QK_NOPE:QK_NOPE + 16], wq[..., QK_NOPE + 16:]
    wuq = jnp.concatenate([qn, q1, q2, q2, q1], axis=-1).reshape(Q_LORA, HEADS * HEAD_PAD)
    wuq_sw = jnp.concatenate([jnp.zeros_like(qn), -q2, q1, q1, -q2], axis=-1).reshape(Q_LORA, HEADS * HEAD_PAD)
    wkv = w_ukv.reshape(KV_LORA, HEADS, QK_NOPE + V_HEAD)
    wk, wv = wkv[..., :QK_NOPE], wkv[..., QK_NOPE:]
    wuk = jnp.concatenate([wk, jnp.zeros_like(wk)], axis=-1).reshape(KV_LORA, HEADS * HEAD_PAD)
    wuv = wv.reshape(KV_LORA, HEADS * V_HEAD)
    wuk_t = wk.transpose(1, 2, 0)
    wabs = jnp.concatenate([wuk_t, jnp.zeros_like(wuk_t)], axis=1)
    eye_h = jnp.eye(HEADS, dtype=F32)
    wuv_bd = (wv.transpose(1, 0, 2)[:, :, None, :] * eye_h[:, None, :, None]).reshape(HEADS * KV_LORA, HEADS * V_HEAD)
    c = lambda w: w.astype(BF16)
    return dict(w_aq=c(w_aq), w_akv=c(w_akv), w_akr=c(w_akr), wkr=c(wkr), wkr_sw=c(wkr_sw), wuq=c(wuq),
                wuq_sw=c(wuq_sw), wuk=c(wuk), wuv=c(wuv), wuk_t=c(wuk_t.reshape(HEADS * QK_NOPE, KV_LORA)),
                wabs=c(wabs), wuv_bd=c(wuv_bd))


def _rope_cos_sin(pos):
    inv = 1.0 / (ROPE_THETA ** (jnp.arange(0, QK_ROPE, 2, dtype=F32) / QK_ROPE))
    ang = pos.astype(F32)[:, None] * inv[None, :]
    return jnp.cos(ang), jnp.sin(ang)


def _rope_tables(pos, g, scale, with_partner):
    c, s = _rope_cos_sin(pos)
    n = pos.shape[0]
    gn, g1, g2 = g[:QK_NOPE], g[QK_NOPE:QK_NOPE + 16], g[QK_NOPE + 16:]
    ones = jnp.ones((n, 1), F32)
    z16, z64 = jnp.zeros((n, 16), F32), jnp.zeros((n, QK_NOPE), F32)
    if with_partner:
        gc = jnp.concatenate([ones * gn, c * g1, c * g2, c * g2, -(c * g1)], axis=1)
        gs = jnp.concatenate([z64, s * g2, s * g1, s * g1, -(s * g2)], axis=1)
    else:
        gc = jnp.concatenate([ones * gn, c * g1, c * g2, z16, z16], axis=1)
        gs = jnp.concatenate([z64, s * g2, s * g1, z16, z16], axis=1)
    return gc * scale, gs * scale


def _mla_proj_kernel(x_ref, g_ref, mod_ref, waq_ref, wakv_ref, wakr_ref, wkr_ref, wkrsw_ref, gqa_ref, gkva_ref,
                     wuq_ref, wuqsw_ref, wuk_ref, wuv_ref, gcq_ref, gsq_ref, gck_ref, gsk_ref,
                     q_ref, k_ref, v_ref, ckv_ref, kr_ref, *, with_kv):
    x = x_ref[...]
    h = _mod_norm(x, g_ref[...], mod_ref[0, 1], mod_ref[0, 0]).astype(BF16)
    aq = _dot(h, waq_ref[...])
    akv = _dot(h, wakv_ref[...])
    kr_ref[...] = _dot(h, wakr_ref[...])
    cq = (aq * lax.rsqrt(jnp.mean(aq * aq, axis=-1, keepdims=True) + EPS) * gqa_ref[...]).astype(BF16)
    ckv = akv * lax.rsqrt(jnp.mean(akv * akv, axis=-1, keepdims=True) + EPS) * gkva_ref[...]
    ckv_ref[...] = ckv
    q_pre = _dot(cq, wuq_ref[...])
    q_sw = _dot(cq, wuqsw_ref[...])
    tm = x.shape[0]
    in_head = lax.broadcasted_iota(jnp.int32, (tm, HEAD_PAD), 1) < QK_HEAD
    gcq, gsq = gcq_ref[...], gsq_ref[...]
    for hd in range(HEADS):
        cols = slice(hd * HEAD_PAD, (hd + 1) * HEAD_PAD)
        qp = q_pre[:, cols]
        ssq = jnp.sum(jnp.where(in_head, qp * qp, 0.0), axis=-1, keepdims=True)
        rinv = lax.rsqrt(ssq * (1.0 / QK_HEAD) + EPS)
        q_ref[:, cols] = (rinv * (qp * gcq + q_sw[:, cols] * gsq)).astype(q_ref.dtype)
    if with_kv:
        ckv_b = ckv.astype(BF16)
        k_pre = _dot(ckv_b, wuk_ref[...])
        v_ref[...] = _dot(ckv_b, wuv_ref[...]).astype(v_ref.dtype)
        kr_slab = _dot(h, wkr_ref[...])
        kr_sw = _dot(h, wkrsw_ref[...])
        gck = gck_ref[...]
        kr_rot = kr_sw * gsk_ref[...]
        for hd in range(HEADS):
            cols = slice(hd * HEAD_PAD, (hd + 1) * HEAD_PAD)
            kp = k_pre[:, cols] + kr_slab
            ssq = jnp.sum(kp * kp, axis=-1, keepdims=True)
            rinv = lax.rsqrt(ssq * (1.0 / QK_HEAD) + EPS)
            k_ref[:, cols] = (rinv * (kp * gck + kr_rot)).astype(k_ref.dtype)
    else:
        k_ref[...] = jnp.zeros_like(k_ref)
        v_ref[...] = jnp.zeros_like(v_ref)


def _mla_proj_call(x, g, mod, mw, g_qa, g_kva, tabs, tiles_per_group, tm, with_kv, q_dtype):
    n, d = x.shape
    gcq, gsq, gck, gsk = tabs
    ntab = gcq.shape[0] // tm
    const = lambda a: pl.BlockSpec(a.shape, lambda i: (0,) * a.ndim)
    tab = pl.BlockSpec((tm, HEAD_PAD), lambda i: (i % ntab, 0))
    row = lambda w: pl.BlockSpec((tm, w), lambda i: (i, 0))
    kv_rows = tm if with_kv else SUBLANES
    kv_n = n if with_kv else SUBLANES * (n // tm)
    kvrow = lambda w: pl.BlockSpec((kv_rows, w), lambda i: (i, 0))
    g_qa2, g_kva2 = g_qa.reshape(1, -1), g_kva.reshape(1, -1)
    ws = [mw["w_aq"], mw["w_akv"], mw["w_akr"], mw["wkr"], mw["wkr_sw"], g_qa2, g_kva2,
          mw["wuq"], mw["wuq_sw"], mw["wuk"], mw["wuv"]]
    return pl.pallas_call(
        functools.partial(_mla_proj_kernel, with_kv=with_kv),
        grid=(n // tm,),
        in_specs=[row(d), pl.BlockSpec((1, d), lambda i: (0, 0)), _mod_spec(mod, tiles_per_group)]
                 + [const(w) for w in ws] + [tab, tab, tab, tab],
        out_specs=[row(HEADS * HEAD_PAD), kvrow(HEADS * HEAD_PAD), kvrow(HEADS * V_HEAD), row(KV_LORA),
                   row(QK_ROPE)],
        out_shape=[jax.ShapeDtypeStruct((n, HEADS * HEAD_PAD), q_dtype),
                   jax.ShapeDtypeStruct((kv_n, HEADS * HEAD_PAD), BF16),
                   jax.ShapeDtypeStruct((kv_n, HEADS * V_HEAD), BF16),
                   jax.ShapeDtypeStruct((n, KV_LORA), F32),
                   jax.ShapeDtypeStruct((n, QK_ROPE), F32)],
        compiler_params=_params("arbitrary"),
        name="mla_proj",
    )(x, g, mod, *ws, gcq, gsq, gck, gsk)


def _flash_kernel(qi_ref, ki_ref, q_ref, k_ref, v_ref, o_ref, m_scr, l_scr, acc_scr):
    p_id = pl.program_id(2)
    qi, ki = qi_ref[p_id], ki_ref[p_id]
    tq, tk = q_ref.shape[0], k_ref.shape[0]

    @pl.when(ki == 0)
    def _():
        m_scr[...] = jnp.full_like(m_scr, NEG_INF)
        l_scr[...] = jnp.zeros_like(l_scr)
        acc_scr[...] = jnp.zeros_like(acc_scr)

    rb = min(tq, FLASH_ROWS)

    def update(masked):
        units = [(i, hh) for i in range(tq // rb) for hh in range(2)]

        def n_keys(i):
            return (i + 1) * rb if masked else tk

        def scores(i, hh):
            cols = slice(hh * HEAD_PAD, (hh + 1) * HEAD_PAD)
            return _dot_nt(q_ref[i * rb:(i + 1) * rb, cols], k_ref[0:n_keys(i), cols])

        def finish(i, hh, s):
            rows = slice(i * rb, (i + 1) * rb)
            nk = n_keys(i)
            if masked:
                visible = (lax.broadcasted_iota(jnp.int32, (rb, nk), 1)
                           <= i * rb + lax.broadcasted_iota(jnp.int32, (rb, nk), 0))
                s = jnp.where(visible, s, NEG_INF)
            m_prev = m_scr[hh, rows, :]
            m_new = jnp.maximum(m_prev, jnp.max(s, axis=1, keepdims=True))
            alpha = jnp.exp2(m_prev - m_new)
            p = jnp.exp2(s - m_new)
            l_scr[hh, rows, :] = alpha * l_scr[hh, rows, :] + jnp.sum(p, axis=1, keepdims=True)
            acc_scr[hh, rows, :] = alpha * acc_scr[hh, rows, :] + _dot(p.astype(BF16), v_ref[0:nk, :])
            m_scr[hh, rows, :] = m_new

        s_next = scores(*units[0])
        for idx, unit in enumerate(units):
            s_cur = s_next
            if idx + 1 < len(units):
                s_next = scores(*units[idx + 1])
            finish(*unit, s_cur)

    @pl.when(ki < qi)
    def _():
        update(False)

    @pl.when(ki == qi)
    def _():
        update(True)

    @pl.when(ki == qi)
    def _():
        lane = lax.broadcasted_iota(jnp.int32, (tq, 2 * V_HEAD), 1)
        o_ref[...] = jnp.where(lane < V_HEAD, acc_scr[0] / l_scr[0], acc_scr[1] / l_scr[1]).astype(o_ref.dtype)


def _flash_call(q, k, v, bsz, tq):
    n = q.shape[0]
    seq = n // bsz
    nq = seq // tq
    pairs = [(a, b) for a in range(nq) for b in range(a + 1)]
    qi_tab = jnp.asarray([p[0] for p in pairs], jnp.int32)
    ki_tab = jnp.asarray([p[1] for p in pairs], jnp.int32)
    grid_spec = pltpu.PrefetchScalarGridSpec(
        num_scalar_prefetch=2,
        grid=(bsz, HEADS // 2, len(pairs)),
        in_specs=[
            pl.BlockSpec((tq, 2 * HEAD_PAD), lambda b, hp, p, qi, ki: (b * nq + qi[p], hp)),
            pl.BlockSpec((tq, 2 * HEAD_PAD), lambda b, hp, p, qi, ki: (b * nq + ki[p], hp)),
            pl.BlockSpec((tq, 2 * V_HEAD), lambda b, hp, p, qi, ki: (b * nq + ki[p], hp)),
        ],
        out_specs=pl.BlockSpec((tq, 2 * V_HEAD), lambda b, hp, p, qi, ki: (b * nq + qi[p], hp)),
        scratch_shapes=[pltpu.VMEM((2, tq, 1), F32), pltpu.VMEM((2, tq, 1), F32),
                        pltpu.VMEM((2, tq, 2 * V_HEAD), F32)],
    )
    return pl.pallas_call(
        _flash_kernel,
        grid_spec=grid_spec,
        out_shape=jax.ShapeDtypeStruct((n, HEADS * V_HEAD), BF16),
        compiler_params=_params("arbitrary", "arbitrary", "arbitrary"),
        name="flash_prompt",
    )(qi_tab, ki_tab, q, k, v)


def _absorb_kernel(q_ref, gk_ref, wabs_ref, o_ref):
    gk = gk_ref[...]
    for hd in range(HEADS):
        qn = (q_ref[:, hd * HEAD_PAD:(hd + 1) * HEAD_PAD] * gk).astype(BF16)
        o_ref[:, hd * KV_LORA:(hd + 1) * KV_LORA] = _dot(qn, wabs_ref[hd]).astype(o_ref.dtype)


def _absorb_call(q, gk_slab, wabs):
    n = q.shape[0]
    return pl.pallas_call(
        _absorb_kernel,
        grid=(1,),
        in_specs=[pl.BlockSpec(q.shape, lambda i: (0, 0)), pl.BlockSpec(gk_slab.shape, lambda i: (0, 0)),
                  pl.BlockSpec(wabs.shape, lambda i: (0, 0, 0))],
        out_specs=pl.BlockSpec((n, HEADS * KV_LORA), lambda i: (0, 0)),
        out_shape=jax.ShapeDtypeStruct((n, HEADS * KV_LORA), BF16),
        compiler_params=_params("arbitrary"),
        name="absorb_q",
    )(q, gk_slab, wabs)


def _chunk_scores(lhs, qx, latb, krt, gt):
    kq = _dot_nt(lhs, latb)
    half = krt.shape[0]
    xt = jnp.concatenate([krt * gt[:half], krt * gt[half:]], axis=0).astype(BF16)
    kx = _dot(qx, xt)
    return kq, kx, jnp.sum(krt * krt, axis=0, keepdims=True)


def _chunk_update(kq, kx, ssq_rope, latb, mask, m_scr, l_scr, acc_scr):
    nk = HEADS * QK_NOPE
    n = latb.shape[0]
    kn = kq[:nk]
    ssq = jnp.sum((kn * kn).reshape(HEADS, QK_NOPE, n), axis=1) + ssq_rope
    rinv = lax.rsqrt(ssq * (1.0 / QK_HEAD) + EPS)
    nrow = kx.shape[0]
    s = (kq[nk:] + kx) * jnp.concatenate([rinv] * (nrow // HEADS), axis=0)
    if mask is not None:
        s = jnp.where(mask, s, NEG_INF)
    m_prev = m_scr[...]
    m_new = jnp.maximum(m_prev, jnp.max(s, axis=1, keepdims=True))
    alpha = jnp.exp2(m_prev - m_new)
    p = jnp.exp2(s - m_new)
    l_scr[...] = alpha * l_scr[...] + jnp.sum(p, axis=1, keepdims=True)
    acc_scr[...] = alpha * acc_scr[...] + _dot(p.astype(BF16), latb)
    m_scr[...] = m_new


def _sample_attn_kernel(pt_ref, qa_ref, qx_ref, wukt_ref, lat_hbm, krt_hbm, gt_ref, latn_ref, krn_ref, gtn_ref, o_ref,
                        lat_buf, krt_buf, lat_sem, krt_sem, lhs_scr, lat_scr, m_scr, l_scr, acc_scr,
                        *, layer, pages, chunk, n_new):
    j = pl.program_id(1)
    nj = pl.num_programs(1)
    step = pl.program_id(0) * nj + j
    slot = step % 2
    nk = wukt_ref.shape[0]
    nrow = qa_ref.shape[0]

    def page_copies(step_, slot_):
        copies = []
        for k in range(pages):
            page = pt_ref[step_ * pages + k]
            copies.append(pltpu.make_async_copy(lat_hbm.at[layer, page],
                                                lat_buf.at[slot_, pl.ds(k * PAGE, PAGE), :], lat_sem.at[slot_]))
            copies.append(pltpu.make_async_copy(krt_hbm.at[layer, page],
                                                krt_buf.at[slot_, :, pl.ds(k * PAGE, PAGE)], krt_sem.at[slot_]))
        return copies

    @pl.when(step == 0)
    def _():
        for cp in page_copies(0, 0):
            cp.start()

    @pl.when(step + 1 < pl.num_programs(0) * nj)
    def _():
        for cp in page_copies(step + 1, 1 - slot):
            cp.start()

    @pl.when(j == 0)
    def _():
        lhs_scr[0:nk, :] = wukt_ref[...]
        lhs_scr[nk:nk + nrow, :] = qa_ref[...]
        m_scr[...] = jnp.full_like(m_scr, NEG_INF)
        l_scr[...] = jnp.zeros_like(l_scr)
        acc_scr[...] = jnp.zeros_like(acc_scr)

    for cp in page_copies(step, slot):
        cp.wait()
    lat_scr[...] = lat_buf[slot].astype(BF16)
    kr_cur = krt_buf.at[slot]
    lhs = lhs_scr[...]
    qx = qx_ref[...]

    def stage_a(c):
        keys = slice(c * chunk, (c + 1) * chunk)
        return _chunk_scores(lhs, qx, lat_scr[keys, :], kr_cur[:, keys], gt_ref[:, keys])

    n_chunks = pages * PAGE // chunk
    nxt = stage_a(0)
    for c in range(n_chunks):
        cur = nxt
        if c + 1 < n_chunks:
            nxt = stage_a(c + 1)
        _chunk_update(*cur, lat_scr[c * chunk:(c + 1) * chunk, :], None, m_scr, l_scr, acc_scr)

    @pl.when(j == pl.num_programs(1) - 1)
    def _():
        t_row = lax.broadcasted_iota(jnp.int32, (nrow, PAGE), 0) // HEADS
        kk = lax.broadcasted_iota(jnp.int32, (nrow, PAGE), 1)
        mask = (kk < n_new) & (kk <= t_row)
        latb = latn_ref[...].astype(BF16)
        _chunk_update(*_chunk_scores(lhs, qx, latb, krn_ref[...], gtn_ref[...]), latb, mask, m_scr, l_scr, acc_scr)
        o_ref[...] = acc_scr[...] / l_scr[...]


def _sample_attn_call(page_table, qabs, qx, wuk_t, cache_latent, cache_krope_t, layer, gt, lat_new, krt_new, gt_new,
                      n_new, pages):
    nb, n_pages = page_table.shape
    nj = n_pages // pages
    nrow = qabs.shape[1]
    chunk = min(pages * PAGE, SAMPLE_CHUNK)
    pt_flat = page_table.reshape(-1)
    keys = pages * PAGE
    grid_spec = pltpu.PrefetchScalarGridSpec(
        num_scalar_prefetch=1,
        grid=(nb, nj),
        in_specs=[
            pl.BlockSpec((None, nrow, KV_LORA), lambda b, j, pt: (b, 0, 0)),
            pl.BlockSpec((None,) + qx.shape[1:], lambda b, j, pt: (b, 0, 0)),
            pl.BlockSpec(wuk_t.shape, lambda b, j, pt: (0, 0)),
            pl.BlockSpec(memory_space=pl.ANY),
            pl.BlockSpec(memory_space=pl.ANY),
            pl.BlockSpec((2 * QK_ROPE, keys), lambda b, j, pt: (0, j)),
            pl.BlockSpec((None, PAGE, KV_LORA), lambda b, j, pt: (b, 0, 0)),
            pl.BlockSpec((None, QK_ROPE, PAGE), lambda b, j, pt: (b, 0, 0)),
            pl.BlockSpec((2 * QK_ROPE, PAGE), lambda b, j, pt: (0, 0)),
        ],
        out_specs=pl.BlockSpec((None, nrow, KV_LORA), lambda b, j, pt: (b, 0, 0)),
        scratch_shapes=[pltpu.VMEM((2, keys, KV_LORA), F32),
                        pltpu.VMEM((2, QK_ROPE, keys), F32),
                        pltpu.SemaphoreType.DMA((2,)),
                        pltpu.SemaphoreType.DMA((2,)),
                        pltpu.VMEM((wuk_t.shape[0] + nrow, KV_LORA), BF16),
                        pltpu.VMEM((keys, KV_LORA), BF16),
                        pltpu.VMEM((nrow, 1), F32), pltpu.VMEM((nrow, 1), F32),
                        pltpu.VMEM((nrow, KV_LORA), F32)],
    )
    return pl.pallas_call(
        functools.partial(_sample_attn_kernel, layer=layer, pages=pages, chunk=chunk, n_new=n_new),
        grid_spec=grid_spec,
        out_shape=jax.ShapeDtypeStruct((nb, nrow, KV_LORA), F32),
        compiler_params=_params("arbitrary", "arbitrary"),
        name="sample_attn",
    )(pt_flat, qabs, qx, wuk_t, cache_latent, cache_krope_t, gt, lat_new, krt_new, gt_new)


def _attn_out_kernel(x_ref, a_ref, mod_ref, *rest, pre_proj):
    if pre_proj:
        wuv_ref, wo_ref, o_ref = rest
        a = _dot(a_ref[...].astype(BF16), wuv_ref[...]).astype(BF16)
    else:
        wo_ref, o_ref = rest
        a = a_ref[...].astype(BF16)
    o_ref[...] = x_ref[...] + mod_ref[0, 2] * _dot(a, wo_ref[...])


def _attn_out_call(x, a, mod, w_o, tiles_per_group, tm, wuv_bd=None):
    n, d = x.shape
    ws = [w_o] if wuv_bd is None else [wuv_bd, w_o]
    return pl.pallas_call(
        functools.partial(_attn_out_kernel, pre_proj=wuv_bd is not None),
        grid=(n // tm,),
        in_specs=[pl.BlockSpec((tm, d), lambda i: (i, 0)), pl.BlockSpec((tm, a.shape[1]), lambda i: (i, 0)),
                  _mod_spec(mod, tiles_per_group)] + [pl.BlockSpec(w.shape, lambda i: (0, 0)) for w in ws],
        out_specs=pl.BlockSpec((tm, d), lambda i: (i, 0)),
        out_shape=jax.ShapeDtypeStruct((n, d), F32),
        compiler_params=_params("arbitrary"),
        name="attn_out",
    )(x, a, mod, *ws)


def kernel(x_prompt, x_sample, c_prompt, c_sample, cache_latent, cache_krope, state_ssm_re, state_ssm_im, page_table, w_ada, b_ada, norm_g, ffn_w_in, ffn_w_out, ssm_w_in, ssm_a_re, ssm_a_im, ssm_log_dt, ssm_b_re, ssm_b_im, ssm_c_re, ssm_c_im, ssm_d, ssm_w_glu, mla_w_a, mla_g_qa, mla_g_kva, mla_w_uq, mla_w_ukv, mla_g_q, mla_g_k, mla_w_o):
    bsz, seq, d = x_prompt.shape
    nb, nt, _ = x_sample.shape
    depth = w_ada.shape[0]
    n_past = page_table.shape[1] * PAGE

    tm_p = _tile(seq, TOKEN_TILE)
    tpg_p = seq // tm_p
    n_s = nb * nt
    s5_t = _tile(seq, S5_CHUNK)
    s5_bs = _tile(nb, S5_SEQS)
    tq = _tile(seq, FLASH_TILE)
    pages = _tile(page_table.shape[1], PAGES_PER_STEP)

    c_all = jnp.concatenate([c_prompt, c_sample], axis=0)
    pad = (-c_all.shape[0]) % SUBLANES
    ada = _ada_call(jnp.pad(c_all, ((0, pad), (0, 0))), w_ada, b_ada)
    ada = ada.reshape(depth, -1, N_ADA, d)

    xp = x_prompt.reshape(bsz * seq, d)
    xs = x_sample.transpose(1, 0, 2).reshape(n_s, d)

    pos_p = jnp.arange(seq, dtype=jnp.int32)
    pos_s = n_past + jnp.arange(nt, dtype=jnp.int32)
    outs = {k: [] for k in ("lat_p", "kr_p", "lat_s", "kr_s", "hre_p", "him_p", "hre_s", "him_s")}
    w_in = ffn_w_in.astype(BF16)
    w_out = ffn_w_out.astype(BF16)

    for i in range(depth):
        ada_p = ada[i, :bsz]
        ada_s = ada[i, bsz:bsz + nb]
        mod_p = [ada_p[:, 3 * j:3 * j + 3, None, :] for j in range(3)]
        mod_s_seq = [ada_s[:, 3 * j:3 * j + 3].transpose(1, 0, 2) for j in range(3)]
        mod_s = [jnp.tile(m, (1, nt, 1))[None] for m in mod_s_seq]
        g = norm_g[i].reshape(3, 1, d)

        xp = _ffn_call(xp, g[0], mod_p[0], w_in, w_out, (i, 0), tpg_p, tm_p)
        xs = _ffn_call(xs, g[0], mod_s[0], w_in, w_out, (i, 0), 1, n_s)

        j = i // 2
        if i % 2 == 0:
            sw = _s5_weights(ssm_a_re[j], ssm_a_im[j], ssm_log_dt[j], ssm_b_re[j], ssm_b_im[j],
                             ssm_c_re[j], ssm_c_im[j], ssm_d[j])
            w_sin = ssm_w_in[j].astype(BF16)
            w_glu = ssm_w_glu[j].astype(BF16)
            xp, hre, him = _s5_prompt_call(xp, g[1], mod_p[1], w_sin, sw, w_glu, bsz, s5_t)
            groups, state = ssm_a_re[j].shape
            outs["hre_p"].append(hre.reshape(bsz, groups, state))
            outs["him_p"].append(him.reshape(bsz, groups, state))
            mod_blk = mod_s_seq[1].reshape(3, nb // s5_bs, s5_bs, d).transpose(1, 0, 2, 3)
            xs3, hre, him = _s5_sample_call(xs.reshape(nt, nb, d), g[1], mod_blk, w_sin, sw, w_glu,
                                            state_ssm_re[j].reshape(nb, -1), state_ssm_im[j].reshape(nb, -1), s5_bs)
            xs = xs3.reshape(n_s, d)
            outs["hre_s"].append(hre.reshape(nb, groups, state))
            outs["him_s"].append(him.reshape(nb, groups, state))
        else:
            mw = _mla_weights(mla_w_a[j], mla_w_uq[j], mla_w_ukv[j])
            w_o = mla_w_o[j].astype(BF16)
            g_q, g_k = mla_g_q[j], mla_g_k[j]
            tabs_p = _rope_tables(pos_p, g_q, ATTN_SCALE * LOG2E, False) + _rope_tables(pos_p, g_k, 1.0, False)
            q, k, v, ckv, kr = _mla_proj_call(xp, g[1], mod_p[1], mw, mla_g_qa[j], mla_g_kva[j], tabs_p, tpg_p, tm_p,
                                              True, BF16)
            outs["lat_p"].append(ckv.reshape(bsz, seq, KV_LORA))
            outs["kr_p"].append(kr.reshape(bsz, seq, QK_ROPE))
            attn = _flash_call(q, k, v, bsz, tq)
            xp = _attn_out_call(xp, attn, mod_p[1], w_o, tpg_p, tm_p)
            pos_rows = jnp.repeat(pos_s, nb)
            tabs_s = _rope_tables(pos_rows, g_q, ATTN_SCALE * LOG2E, True) + _rope_tables(pos_rows, g_k, 1.0, False)
            q_s, _, _, ckv_s, kr_s = _mla_proj_call(xs, g[1], mod_s[1], mw, mla_g_qa[j], mla_g_kva[j], tabs_s, 1, n_s,
                                                    False, F32)
            lat_new = ckv_s.reshape(nt, nb, KV_LORA).transpose(1, 0, 2)
            kr_new = kr_s.reshape(nt, nb, QK_ROPE).transpose(1, 0, 2)
            outs["lat_s"].append(lat_new)
            outs["kr_s"].append(kr_new)
            q_sm = q_s.reshape(nt, nb, HEADS * HEAD_PAD).transpose(1, 0, 2).reshape(n_s, HEADS * HEAD_PAD)
            gk_slab = jnp.concatenate([g_k[:QK_NOPE], jnp.zeros((HEAD_PAD - QK_NOPE,), F32)]).reshape(1, HEAD_PAD)
            qabs = _absorb_call(q_sm, gk_slab, mw["wabs"]).reshape(nb, nt * HEADS, KV_LORA)
            qx = q_sm.reshape(nb, nt * HEADS, HEAD_PAD)[..., QK_NOPE:].astype(BF16)
            pos_k = jnp.arange(n_past + PAGE, dtype=jnp.int32)
            ck, sk = _rope_cos_sin(pos_k)
            g1, g2 = g_k[QK_NOPE:QK_NOPE + 16], g_k[QK_NOPE + 16:]
            gt = jnp.concatenate([ck * g1, ck * g2, sk * g1, sk * g2], axis=1).T
            padn = ((0, 0), (0, PAGE - nt), (0, 0))
            krt_new = jnp.pad(kr_new, padn).transpose(0, 2, 1)
            ctx = _sample_attn_call(page_table, qabs, qx, mw["wuk_t"], cache_latent, cache_krope.transpose(0, 1, 3, 2),
                                    j, gt[:, :n_past], jnp.pad(lat_new, padn), krt_new, gt[:, n_past:], nt, pages)
            ctx_tm = ctx.reshape(nb, nt, HEADS * KV_LORA).transpose(1, 0, 2).reshape(n_s, HEADS * KV_LORA)
            xs = _attn_out_call(xs, ctx_tm, mod_s[1], w_o, 1, n_s, wuv_bd=mw["wuv_bd"])

        xp = _ffn_call(xp, g[2], mod_p[2], w_in, w_out, (i, 1), tpg_p, tm_p)
        xs = _ffn_call(xs, g[2], mod_s[2], w_in, w_out, (i, 1), 1, n_s)

    y_prompt = xp.reshape(bsz, seq, d)
    y_sample = xs.reshape(nt, nb, d).transpose(1, 0, 2)
    st = lambda k: jnp.stack(outs[k])
    return (y_prompt, y_sample, st("lat_p"), st("kr_p"), st("lat_s"), st("kr_s"),
            st("hre_p"), st("him_p"), st("hre_s"), st("him_s"))
```

```python
import functools
import math

import jax
import jax.numpy as jnp
from jax import lax
from jax.experimental import pallas as pl
from jax.experimental.pallas import tpu as pltpu

F32 = jnp.float32
BF16 = jnp.bfloat16

EPS = 1e-6
NEG_INF = -1e30
N_ADA = 9
PAGE = 128
SSM_GROUP = 16
HEADS = 16
Q_LORA = 384
KV_LORA = 256
QK_NOPE = 64
QK_ROPE = 32
QK_HEAD = QK_NOPE + QK_ROPE
V_HEAD = 64
HEAD_PAD = 128
ROPE_THETA = 10000.0
ATTN_SCALE = 1.0 / math.sqrt(QK_HEAD)
LOG2E = math.log2(math.e)
DT_MIN = 1e-3
DT_MAX = 1e-1

LANES = 128
SUBLANES = 8
VMEM_LIMIT = 56 * 1024 * 1024

TOKEN_TILE = 512
FFN_CHUNK = 256
S5_CHUNK = 256
S5_SEQS = 64
FLASH_TILE = 1024
FLASH_ROWS = 256
PAGES_PER_STEP = 32
SAMPLE_CHUNK = 512


def _params(*sem):
    return pltpu.CompilerParams(dimension_semantics=sem, vmem_limit_bytes=VMEM_LIMIT)


def _dot(a, b):
    return jnp.dot(a, b, preferred_element_type=F32)


def _dot_nt(a, b):
    return lax.dot_general(a, b, (((1,), (1,)), ((), ())), preferred_element_type=F32)


def _mod_norm(x, g, scale, shift):
    ms = jnp.mean(x * x, axis=-1, keepdims=True)
    y = x * lax.rsqrt(ms + EPS) * g
    return y * (1.0 + scale) + shift


def _ada_kernel(c_ref, w_ref, b_ref, o_ref):
    c = c_ref[...]
    h = (c * jax.nn.sigmoid(c)).astype(BF16)
    o_ref[0] = _dot(h, w_ref[0].astype(BF16)) + b_ref[0]


def _ada_call(c_all, w_ada, b_ada):
    depth, d, n = w_ada.shape
    rows = c_all.shape[0]
    tn = 1152 if n % 1152 == 0 else n
    return pl.pallas_call(
        _ada_kernel,
        grid=(depth, n // tn),
        in_specs=[
            pl.BlockSpec((rows, d), lambda l, j: (0, 0)),
            pl.BlockSpec((1, d, tn), lambda l, j: (l, 0, j)),
            pl.BlockSpec((1, 1, tn), lambda l, j: (l, 0, j)),
        ],
        out_specs=pl.BlockSpec((1, rows, tn), lambda l, j: (l, 0, j)),
        out_shape=jax.ShapeDtypeStruct((depth, rows, n), F32),
        compiler_params=_params("arbitrary", "arbitrary"),
        name="ada",
    )(c_all, w_ada, b_ada.reshape(depth, 1, n))


def _mod_spec(mod, tiles_per_group):
    _, three, rows, d = mod.shape
    return pl.BlockSpec((1, three, rows, d), lambda i, *_: (i // tiles_per_group, 0, 0, 0))


def _tile(n, pref):
    return pref if n % pref == 0 else n


def _ffn_kernel(x_ref, g_ref, mod_ref, win_ref, wout_ref, o_ref, *, fc):
    x = x_ref[...]
    h = _mod_norm(x, g_ref[...], mod_ref[0, 1], mod_ref[0, 0]).astype(BF16)
    f = wout_ref.shape[0]

    def gate_up(c):
        return (_dot(h, win_ref[:, c * fc:(c + 1) * fc]), _dot(h, win_ref[:, f + c * fc:f + (c + 1) * fc]))

    acc = None
    nxt = gate_up(0)
    for c in range(f // fc):
        a, u = nxt
        if (c + 1) * fc < f:
            nxt = gate_up(c + 1)
        act = (a * jax.nn.sigmoid(a) * u).astype(BF16)
        part = _dot(act, wout_ref[c * fc:(c + 1) * fc, :])
        acc = part if acc is None else acc + part
    o_ref[...] = x + (0.5 * mod_ref[0, 2]) * acc


def _ffn_call(x, g, mod, w_in, w_out, which, tiles_per_group, tm):
    n, d = x.shape
    f = w_out.shape[2]
    fc = FFN_CHUNK if f % FFN_CHUNK == 0 else f
    layer, k = which
    return pl.pallas_call(
        functools.partial(_ffn_kernel, fc=fc),
        grid=(n // tm,),
        in_specs=[
            pl.BlockSpec((tm, d), lambda i: (i, 0)),
            pl.BlockSpec((1, d), lambda i: (0, 0)),
            _mod_spec(mod, tiles_per_group),
            pl.BlockSpec((None, None, d, 2 * f), lambda i: (layer, k, 0, 0)),
            pl.BlockSpec((None, None, f, d), lambda i: (layer, k, 0, 0)),
        ],
        out_specs=pl.BlockSpec((tm, d), lambda i: (i, 0)),
        out_shape=jax.ShapeDtypeStruct((n, d), F32),
        compiler_params=_params("arbitrary"),
        name="ffn",
    )(x, g, mod, w_in, w_out)


def _s5_discretize(a_re, a_im, log_dt, b_re, b_im):
    dt = jnp.exp(log_dt)[:, None]
    mag = jnp.exp(a_re * dt)
    ab_re, ab_im = mag * jnp.cos(a_im * dt), mag * jnp.sin(a_im * dt)
    den = a_re * a_re + a_im * a_im
    n_re, n_im = ab_re - 1.0, ab_im
    w_re = (n_re * a_re + n_im * a_im) / den
    w_im = (n_im * a_re - n_re * a_im) / den
    bb_re = w_re[..., None] * b_re - w_im[..., None] * b_im
    bb_im = w_re[..., None] * b_im + w_im[..., None] * b_re
    return ab_re, ab_im, bb_re, bb_im


def _s5_weights(a_re, a_im, log_dt, b_re, b_im, c_re, c_im, d_skip):
    groups, state = a_re.shape
    ab_re, ab_im, bb_re, bb_im = _s5_discretize(a_re, a_im, log_dt, b_re, b_im)
    nkt = groups // 16
    eye16 = jnp.eye(16, dtype=F32)

    def bd(bb):
        t = bb.transpose(0, 2, 1).reshape(nkt, 16, SSM_GROUP, state)
        full = t[:, :, :, None, :] * eye16[None, :, None, :, None]
        return full.reshape(nkt, 16 * SSM_GROUP, 16 * state).astype(BF16)

    nj = groups // 8
    sel = (jnp.arange(16)[None, None, :] ==
           (jnp.arange(8)[None, :, None] + 8 * (jnp.arange(nj)[:, None, None] % 2))).astype(F32)

    def cd(cc):
        t = cc.transpose(0, 2, 1).reshape(nj, 8, state, SSM_GROUP)
        full = t[:, :, :, None, :] * sel[:, :, None, :, None]
        return full.reshape(nj, 8 * state, 16 * SSM_GROUP).astype(BF16)

    return dict(bd_re=bd(bb_re), bd_im=bd(bb_im), cd_re=cd(c_re), cd_im=cd(c_im),
                a_re=ab_re.reshape(1, groups * state), a_im=ab_im.reshape(1, groups * state),
                dd=d_skip.reshape(1, -1))


def _s5_tail(x, u, y_parts, dd, gate, wglu):
    y = jnp.concatenate(y_parts, axis=1) + dd * u
    z = _dot(jax.nn.gelu(y).astype(BF16), wglu)
    half = z.shape[1] // 2
    return x + gate * (z[:, :half] * jax.nn.sigmoid(z[:, half:]))


def _s5_prompt_kernel(x_ref, g_ref, mod_ref, win_ref, bdre_ref, bdim_ref, are_ref, aim_ref, cdre_ref, cdim_ref,
                      dd_ref, wglu_ref, o_ref, hre_ref, him_ref, sre_scr, sim_scr, cre_scr, cim_scr, *, t_len, pitch):
    @pl.when(pl.program_id(1) == 0)
    def _():
        cre_scr[...] = jnp.zeros_like(cre_scr)
        cim_scr[...] = jnp.zeros_like(cim_scr)

    x = x_ref[...]
    h = _mod_norm(x, g_ref[...], mod_ref[0, 1], mod_ref[0, 0]).astype(BF16)
    u = _dot(h, win_ref[...])
    ub = u.astype(BF16)
    nkt, kw, sw = bdre_ref.shape
    per_kt = sw // LANES
    nvreg = nkt * per_kt // SUBLANES
    for kt in range(nkt):
        ukt = ub[:, kt * kw:(kt + 1) * kw]
        bre = _dot(ukt, bdre_ref[kt])
        bim = _dot(ukt, bdim_ref[kt])
        for m in range(per_kt):
            r0 = (kt * per_kt + m) * pitch
            sre_scr[r0:r0 + t_len, :] = bre[:, m * LANES:(m + 1) * LANES]
            sim_scr[r0:r0 + t_len, :] = bim[:, m * LANES:(m + 1) * LANES]

    vrows = lambda k: slice(k * SUBLANES, (k + 1) * SUBLANES)
    a_re = [are_ref[vrows(k), :] for k in range(nvreg)]
    a_im = [aim_ref[vrows(k), :] for k in range(nvreg)]

    def step(t, carry):
        new = []
        for k in range(nvreg):
            sr, si = carry[2 * k], carry[2 * k + 1]
            rows = pl.ds(k * SUBLANES * pitch + t, SUBLANES, stride=pitch)
            nr = a_re[k] * sr - a_im[k] * si + sre_scr[rows, :]
            ni = a_re[k] * si + a_im[k] * sr + sim_scr[rows, :]
            sre_scr[rows, :] = nr
            sim_scr[rows, :] = ni
            new += [nr, ni]
        return tuple(new)

    init = []
    for k in range(nvreg):
        init += [cre_scr[vrows(k), :], cim_scr[vrows(k), :]]
    fin = lax.fori_loop(0, t_len, step, tuple(init), unroll=8)
    for k in range(nvreg):
        cre_scr[vrows(k), :] = fin[2 * k]
        cim_scr[vrows(k), :] = fin[2 * k + 1]
        hre_ref[0, vrows(k), :] = fin[2 * k]
        him_ref[0, vrows(k), :] = fin[2 * k + 1]

    y_parts = []
    for kt in range(nkt):
        slabs = lambda scr: jnp.concatenate(
            [scr[(kt * per_kt + m) * pitch:(kt * per_kt + m) * pitch + t_len, :] for m in range(per_kt)], axis=1)
        y_parts.append(_dot(slabs(sre_scr).astype(BF16), cdre_ref[kt]) - _dot(slabs(sim_scr).astype(BF16), cdim_ref[kt]))
    o_ref[...] = _s5_tail(x, u, y_parts, dd_ref[...], mod_ref[0, 2], wglu_ref[...])


def _s5_prompt_call(x, g, mod, w_in, sw, w_glu, bsz, t_len):
    n, d = x.shape
    seq = n // bsz
    nt = seq // t_len
    pitch = t_len + SUBLANES
    nstate = sw["a_re"].shape[1]
    nslab = nstate // LANES
    nkt = sw["bd_re"].shape[0]
    a_re = sw["a_re"].reshape(nslab, LANES)
    a_im = sw["a_im"].reshape(nslab, LANES)
    cd_re = sw["cd_re"].reshape(nkt, nstate // nkt, -1)
    cd_im = sw["cd_im"].reshape(nkt, nstate // nkt, -1)
    const = lambda *shape: pl.BlockSpec(shape, lambda b, t: (0,) * len(shape))
    out, hre, him = pl.pallas_call(
        functools.partial(_s5_prompt_kernel, t_len=t_len, pitch=pitch),
        grid=(bsz, nt),
        in_specs=[
            pl.BlockSpec((t_len, d), lambda b, t: (b * nt + t, 0)),
            const(1, d),
            pl.BlockSpec((1, 3, 1, d), lambda b, t: (b, 0, 0, 0)),
            const(*w_in.shape),
            const(*sw["bd_re"].shape), const(*sw["bd_im"].shape),
            const(nslab, LANES), const(nslab, LANES),
            const(*cd_re.shape), const(*cd_im.shape),
            const(1, d),
            const(*w_glu.shape),
        ],
        out_specs=[
            pl.BlockSpec((t_len, d), lambda b, t: (b * nt + t, 0)),
            pl.BlockSpec((1, nslab, LANES), lambda b, t: (b, 0, 0)),
            pl.BlockSpec((1, nslab, LANES), lambda b, t: (b, 0, 0)),
        ],
        out_shape=[jax.ShapeDtypeStruct((n, d), F32),
                   jax.ShapeDtypeStruct((bsz, nslab, LANES), F32),
                   jax.ShapeDtypeStruct((bsz, nslab, LANES), F32)],
        scratch_shapes=[pltpu.VMEM((nslab * pitch, LANES), F32), pltpu.VMEM((nslab * pitch, LANES), F32),
                        pltpu.VMEM((nslab, LANES), F32), pltpu.VMEM((nslab, LANES), F32)],
        compiler_params=_params("arbitrary", "arbitrary"),
        name="s5_prompt",
    )(x, g, mod, w_in, sw["bd_re"], sw["bd_im"], a_re, a_im, cd_re, cd_im, sw["dd"], w_glu)
    return out, hre, him


def _s5_sample_kernel(x_ref, g_ref, mod_ref, win_ref, bdre_ref, bdim_ref, are_ref, aim_ref, h0re_ref, h0im_ref,
                      cdre_ref, cdim_ref, dd_ref, wglu_ref, o_ref, hre_ref, him_ref, sre_scr, sim_scr):
    nt, bs, d = x_ref.shape
    x = x_ref[...].reshape(nt * bs, d)
    mods = [jnp.concatenate([mod_ref[0, k]] * nt, axis=0) for k in range(3)]
    h = _mod_norm(x, g_ref[...], mods[1], mods[0]).astype(BF16)
    u = _dot(h, win_ref[...])
    ub = u.astype(BF16)
    nkt = bdre_ref.shape[0]
    kw = bdre_ref.shape[1]
    sw = bdre_ref.shape[2]
    for kt in range(nkt):
        ukt = ub[:, kt * kw:(kt + 1) * kw]
        sre_scr[:, kt * sw:(kt + 1) * sw] = _dot(ukt, bdre_ref[kt])
        sim_scr[:, kt * sw:(kt + 1) * sw] = _dot(ukt, bdim_ref[kt])
    a_re = are_ref[...]
    a_im = aim_ref[...]
    sr = h0re_ref[...]
    si = h0im_ref[...]
    for t in range(nt):
        rows = slice(t * bs, (t + 1) * bs)
        nr = a_re * sr - a_im * si + sre_scr[rows, :]
        ni = a_re * si + a_im * sr + sim_scr[rows, :]
        sre_scr[rows, :] = nr
        sim_scr[rows, :] = ni
        sr, si = nr, ni
    hre_ref[...] = sr
    him_ref[...] = si
    y_parts = []
    for kt in range(nkt):
        cols = slice(kt * sw, (kt + 1) * sw)
        y_parts.append(_dot(sre_scr[:, cols].astype(BF16), cdre_ref[kt])
                       - _dot(sim_scr[:, cols].astype(BF16), cdim_ref[kt]))
    out = _s5_tail(x, u, y_parts, dd_ref[...], mods[2], wglu_ref[...])
    o_ref[...] = out.reshape(nt, bs, d)


def _s5_sample_call(x_tm, g, mod, w_in, sw, w_glu, h0_re, h0_im, bs):
    nt, nb, d = x_tm.shape
    nstate = sw["a_re"].shape[1]
    nkt = sw["bd_re"].shape[0]
    cd_re = sw["cd_re"].reshape(nkt, nstate // nkt, -1)
    cd_im = sw["cd_im"].reshape(nkt, nstate // nkt, -1)
    const = lambda *shape: pl.BlockSpec(shape, lambda i: (0,) * len(shape))
    return pl.pallas_call(
        _s5_sample_kernel,
        grid=(nb // bs,),
        in_specs=[
            pl.BlockSpec((nt, bs, d), lambda i: (0, i, 0)),
            const(1, d),
            pl.BlockSpec((1, 3, bs, d), lambda i: (i, 0, 0, 0)),
            const(*w_in.shape),
            const(*sw["bd_re"].shape), const(*sw["bd_im"].shape),
            const(1, nstate), const(1, nstate),
            pl.BlockSpec((bs, nstate), lambda i: (i, 0)),
            pl.BlockSpec((bs, nstate), lambda i: (i, 0)),
            const(*cd_re.shape), const(*cd_im.shape),
            const(1, d),
            const(*w_glu.shape),
        ],
        out_specs=[
            pl.BlockSpec((nt, bs, d), lambda i: (0, i, 0)),
            pl.BlockSpec((bs, nstate), lambda i: (i, 0)),
            pl.BlockSpec((bs, nstate), lambda i: (i, 0)),
        ],
        out_shape=[jax.ShapeDtypeStruct((nt, nb, d), F32),
                   jax.ShapeDtypeStruct((nb, nstate), F32),
                   jax.ShapeDtypeStruct((nb, nstate), F32)],
        scratch_shapes=[pltpu.VMEM((nt * bs, nstate), F32), pltpu.VMEM((nt * bs, nstate), F32)],
        compiler_params=_params("arbitrary"),
        name="s5_sample",
    )(x_tm, g, mod, w_in, sw["bd_re"], sw["bd_im"], sw["a_re"], sw["a_im"], h0_re, h0_im, cd_re, cd_im,
      sw["dd"], w_glu)


def _mla_weights(w_a, w_uq, w_ukv):
    d = w_a.shape[0]
    w_aq = w_a[:, :Q_LORA]
    w_akv = w_a[:, Q_LORA:Q_LORA + KV_LORA]
    w_akr = w_a[:, Q_LORA + KV_LORA:]
    kr1, kr2 = w_akr[:, :16], w_akr[:, 16:]
    z64, z32 = jnp.zeros((d, QK_NOPE), F32), jnp.zeros((d, 32), F32)
    wkr = jnp.concatenate([z64, kr1, kr2, z32], axis=1)
    wkr_sw = jnp.concatenate([z64, -kr2, kr1, z32], axis=1)
    wq = w_uq.reshape(Q_LORA, HEADS, QK_HEAD)
    qn, q1, q2 = wq[..., :QK_NOPE], wq[..., QK_NOPE:QK_NOPE + 16], wq[..., QK_NOPE + 16:]
    wuq = jnp.concatenate([qn, q1, q2, q2, q1], axis=-1).reshape(Q_LORA, HEADS * HEAD_PAD)
    wuq_sw = jnp.concatenate([jnp.zeros_like(qn), -q2, q1, q1, -q2], axis=-1).reshape(Q_LORA, HEADS * HEAD_PAD)
    wkv = w_ukv.reshape(KV_LORA, HEADS, QK_NOPE + V_HEAD)
    wk, wv = wkv[..., :QK_NOPE], wkv[..., QK_NOPE:]
    wuk = jnp.concatenate([wk, jnp.zeros_like(wk)], axis=-1).reshape(KV_LORA, HEADS * HEAD_PAD)
    wuv = wv.reshape(KV_LORA, HEADS * V_HEAD)
    wuk_t = wk.transpose(1, 2, 0)
    wabs = jnp.concatenate([wuk_t, jnp.zeros_like(wuk_t)], axis=1)
    eye_h = jnp.eye(HEADS, dtype=F32)
    wuv_bd = (wv.transpose(1, 0, 2)[:, :, None, :] * eye_h[:, None, :, None]).reshape(HEADS * KV_LORA, HEADS * V_HEAD)
    c = lambda w: w.astype(BF16)
    return dict(w_aq=c(w_aq), w_akv=c(w_akv), w_akr=c(w_akr), wkr=c(wkr), wkr_sw=c(wkr_sw), wuq=c(wuq),
                wuq_sw=c(wuq_sw), wuk=c(wuk), wuv=c(wuv), wuk_t=c(wuk_t.reshape(HEADS * QK_NOPE, KV_LORA)),
                wabs=c(wabs), wuv_bd=c(wuv_bd))


def _rope_cos_sin(pos):
    inv = 1.0 / (ROPE_THETA ** (jnp.arange(0, QK_ROPE, 2, dtype=F32) / QK_ROPE))
    ang = pos.astype(F32)[:, None] * inv[None, :]
    return jnp.cos(ang), jnp.sin(ang)


def _rope_tables(pos, g, scale, with_partner):
    c, s = _rope_cos_sin(pos)
    n = pos.shape[0]
    gn, g1, g2 = g[:QK_NOPE], g[QK_NOPE:QK_NOPE + 16], g[QK_NOPE + 16:]
    ones = jnp.ones((n, 1), F32)
    z16, z64 = jnp.zeros((n, 16), F32), jnp.zeros((n, QK_NOPE), F32)
    if with_partner:
        gc = jnp.concatenate([ones * gn, c * g1, c * g2, c * g2, -(c * g1)], axis=1)
        gs = jnp.concatenate([z64, s * g2, s * g1, s * g1, -(s * g2)], axis=1)
    else:
        gc = jnp.concatenate([ones * gn, c * g1, c * g2, z16, z16], axis=1)
        gs = jnp.concatenate([z64, s * g2, s * g1, z16, z16], axis=1)
    return gc * scale, gs * scale


def _mla_proj_kernel(x_ref, g_ref, mod_ref, waq_ref, wakv_ref, wakr_ref, wkr_ref, wkrsw_ref, gqa_ref, gkva_ref,
                     wuq_ref, wuqsw_ref, wuk_ref, wuv_ref, gcq_ref, gsq_ref, gck_ref, gsk_ref,
                     q_ref, k_ref, v_ref, ckv_ref, kr_ref, *, with_kv):
    x = x_ref[...]
    h = _mod_norm(x, g_ref[...], mod_ref[0, 1], mod_ref[0, 0]).astype(BF16)
    aq = _dot(h, waq_ref[...])
    akv = _dot(h, wakv_ref[...])
    kr_ref[...] = _dot(h, wakr_ref[...])
    cq = (aq * lax.rsqrt(jnp.mean(aq * aq, axis=-1, keepdims=True) + EPS) * gqa_ref[...]).astype(BF16)
    ckv = akv * lax.rsqrt(jnp.mean(akv * akv, axis=-1, keepdims=True) + EPS) * gkva_ref[...]
    ckv_ref[...] = ckv
    q_pre = _dot(cq, wuq_ref[...])
    q_sw = _dot(cq, wuqsw_ref[...])
    tm = x.shape[0]
    in_head = lax.broadcasted_iota(jnp.int32, (tm, HEAD_PAD), 1) < QK_HEAD
    gcq, gsq = gcq_ref[...], gsq_ref[...]
    for hd in range(HEADS):
        cols = slice(hd * HEAD_PAD, (hd + 1) * HEAD_PAD)
        qp = q_pre[:, cols]
        ssq = jnp.sum(jnp.where(in_head, qp * qp, 0.0), axis=-1, keepdims=True)
        rinv = lax.rsqrt(ssq * (1.0 / QK_HEAD) + EPS)
        q_ref[:, cols] = (rinv * (qp * gcq + q_sw[:, cols] * gsq)).astype(q_ref.dtype)
    if with_kv:
        ckv_b = ckv.astype(BF16)
        k_pre = _dot(ckv_b, wuk_ref[...])
        v_all = _dot(ckv_b, wuv_ref[...])
        pair_w = 2 * V_HEAD
        one_col = (lax.broadcasted_iota(jnp.int32, (tm, pair_w), 1) == 0).astype(v_ref.dtype)
        for hp in range(HEADS // 2):
            v_ref[:, hp * 2 * pair_w:hp * 2 * pair_w + pair_w] = v_all[:, hp * pair_w:(hp + 1) * pair_w].astype(v_ref.dtype)
            v_ref[:, hp * 2 * pair_w + pair_w:(hp + 1) * 2 * pair_w] = one_col
        kr_slab = _dot(h, wkr_ref[...])
        kr_sw = _dot(h, wkrsw_ref[...])
        gck = gck_ref[...]
        kr_rot = kr_sw * gsk_ref[...]
        for hd in range(HEADS):
            cols = slice(hd * HEAD_PAD, (hd + 1) * HEAD_PAD)
            kp = k_pre[:, cols] + kr_slab
            ssq = jnp.sum(kp * kp, axis=-1, keepdims=True)
            rinv = lax.rsqrt(ssq * (1.0 / QK_HEAD) + EPS)
            k_ref[:, cols] = (rinv * (kp * gck + kr_rot)).astype(k_ref.dtype)
    else:
        k_ref[...] = jnp.zeros_like(k_ref)
        v_ref[...] = jnp.zeros_like(v_ref)


def _mla_proj_call(x, g, mod, mw, g_qa, g_kva, tabs, tiles_per_group, tm, with_kv, q_dtype):
    n, d = x.shape
    gcq, gsq, gck, gsk = tabs
    ntab = gcq.shape[0] // tm
    const = lambda a: pl.BlockSpec(a.shape, lambda i: (0,) * a.ndim)
    tab = pl.BlockSpec((tm, HEAD_PAD), lambda i: (i % ntab, 0))
    row = lambda w: pl.BlockSpec((tm, w), lambda i: (i, 0))
    kv_rows = tm if with_kv else SUBLANES
    kv_n = n if with_kv else SUBLANES * (n // tm)
    kvrow = lambda w: pl.BlockSpec((kv_rows, w), lambda i: (i, 0))
    g_qa2, g_kva2 = g_qa.reshape(1, -1), g_kva.reshape(1, -1)
    ws = [mw["w_aq"], mw["w_akv"], mw["w_akr"], mw["wkr"], mw["wkr_sw"], g_qa2, g_kva2,
          mw["wuq"], mw["wuq_sw"], mw["wuk"], mw["wuv"]]
    return pl.pallas_call(
        functools.partial(_mla_proj_kernel, with_kv=with_kv),
        grid=(n // tm,),
        in_specs=[row(d), pl.BlockSpec((1, d), lambda i: (0, 0)), _mod_spec(mod, tiles_per_group)]
                 + [const(w) for w in ws] + [tab, tab, tab, tab],
        out_specs=[row(HEADS * HEAD_PAD), kvrow(HEADS * HEAD_PAD), kvrow(2 * HEADS * V_HEAD), row(KV_LORA),
                   row(QK_ROPE)],
        out_shape=[jax.ShapeDtypeStruct((n, HEADS * HEAD_PAD), q_dtype),
                   jax.ShapeDtypeStruct((kv_n, HEADS * HEAD_PAD), BF16),
                   jax.ShapeDtypeStruct((kv_n, 2 * HEADS * V_HEAD), BF16),
                   jax.ShapeDtypeStruct((n, KV_LORA), F32),
                   jax.ShapeDtypeStruct((n, QK_ROPE), F32)],
        compiler_params=_params("arbitrary"),
        name="mla_proj",
    )(x, g, mod, *ws, gcq, gsq, gck, gsk)


def _flash_kernel(qi_ref, ki_ref, q_ref, k_ref, v_ref, o_ref, m_scr, acc_scr):
    p_id = pl.program_id(2)
    qi, ki = qi_ref[p_id], ki_ref[p_id]
    tq, tk = q_ref.shape[0], k_ref.shape[0]

    @pl.when(ki == 0)
    def _():
        m_scr[...] = jnp.full_like(m_scr, NEG_INF)
        acc_scr[...] = jnp.zeros_like(acc_scr)

    rb = min(tq, FLASH_ROWS)

    def update(masked):
        units = [(i, hh) for i in range(tq // rb) for hh in range(2)]

        def n_keys(i):
            return (i + 1) * rb if masked else tk

        def scores(i, hh):
            cols = slice(hh * HEAD_PAD, (hh + 1) * HEAD_PAD)
            return _dot_nt(q_ref[i * rb:(i + 1) * rb, cols], k_ref[0:n_keys(i), cols])

        def finish(i, hh, s):
            rows = slice(i * rb, (i + 1) * rb)
            nk = n_keys(i)
            if masked:
                visible = (lax.broadcasted_iota(jnp.int32, (rb, nk), 1)
                           <= i * rb + lax.broadcasted_iota(jnp.int32, (rb, nk), 0))
                s = jnp.where(visible, s, NEG_INF)
            m_prev = m_scr[hh, rows, :]
            m_new = jnp.maximum(m_prev, jnp.max(s, axis=1, keepdims=True))
            alpha = jnp.exp2(m_prev - m_new)
            p = jnp.exp2((s - m_new).astype(BF16))
            acc_scr[hh, rows, :] = alpha * acc_scr[hh, rows, :] + _dot(p, v_ref[0:nk, :])
            m_scr[hh, rows, :] = m_new

        s_next = scores(*units[0])
        for idx, unit in enumerate(units):
            s_cur = s_next
            if idx + 1 < len(units):
                s_next = scores(*units[idx + 1])
            finish(*unit, s_cur)

    @pl.when(ki < qi)
    def _():
        update(False)

    @pl.when(ki == qi)
    def _():
        update(True)

    @pl.when(ki == qi)
    def _():
        pair_w = 2 * V_HEAD
        lane = lax.broadcasted_iota(jnp.int32, (tq, pair_w), 1)
        out = [acc_scr[hh, :, 0:pair_w] / acc_scr[hh, :, pair_w:pair_w + 1] for hh in range(2)]
        o_ref[...] = jnp.where(lane < V_HEAD, out[0], out[1]).astype(o_ref.dtype)


def _flash_call(q, k, v, bsz, tq):
    n = q.shape[0]
    seq = n // bsz
    nq = seq // tq
    pairs = [(a, b) for a in range(nq) for b in range(a + 1)]
    qi_tab = jnp.asarray([p[0] for p in pairs], jnp.int32)
    ki_tab = jnp.asarray([p[1] for p in pairs], jnp.int32)
    grid_spec = pltpu.PrefetchScalarGridSpec(
        num_scalar_prefetch=2,
        grid=(bsz, HEADS // 2, len(pairs)),
        in_specs=[
            pl.BlockSpec((tq, 2 * HEAD_PAD), lambda b, hp, p, qi, ki: (b * nq + qi[p], hp)),
            pl.BlockSpec((tq, 2 * HEAD_PAD), lambda b, hp, p, qi, ki: (b * nq + ki[p], hp)),
            pl.BlockSpec((tq, 4 * V_HEAD), lambda b, hp, p, qi, ki: (b * nq + ki[p], hp)),
        ],
        out_specs=pl.BlockSpec((tq, 2 * V_HEAD), lambda b, hp, p, qi, ki: (b * nq + qi[p], hp)),
        scratch_shapes=[pltpu.VMEM((2, tq, 1), F32), pltpu.VMEM((2, tq, 4 * V_HEAD), F32)],
    )
    return pl.pallas_call(
        _flash_kernel,
        grid_spec=grid_spec,
        out_shape=jax.ShapeDtypeStruct((n, HEADS * V_HEAD), BF16),
        compiler_params=_params("arbitrary", "arbitrary", "arbitrary"),
        name="flash_prompt",
    )(qi_tab, ki_tab, q, k, v)


def _absorb_kernel(q_ref, gk_ref, wabs_ref, o_ref):
    gk = gk_ref[...]
    for hd in range(HEADS):
        qn = (q_ref[:, hd * HEAD_PAD:(hd + 1) * HEAD_PAD] * gk).astype(BF16)
        o_ref[:, hd * KV_LORA:(hd + 1) * KV_LORA] = _dot(qn, wabs_ref[hd]).astype(o_ref.dtype)


def _absorb_call(q, gk_slab, wabs):
    n = q.shape[0]
    return pl.pallas_call(
        _absorb_kernel,
        grid=(1,),
        in_specs=[pl.BlockSpec(q.shape, lambda i: (0, 0)), pl.BlockSpec(gk_slab.shape, lambda i: (0, 0)),
                  pl.BlockSpec(wabs.shape, lambda i: (0, 0, 0))],
        out_specs=pl.BlockSpec((n, HEADS * KV_LORA), lambda i: (0, 0)),
        out_shape=jax.ShapeDtypeStruct((n, HEADS * KV_LORA), BF16),
        compiler_params=_params("arbitrary"),
        name="absorb_q",
    )(q, gk_slab, wabs)


def _chunk_scores(lhs, qx, latb, krt, gt):
    kq = _dot_nt(lhs, latb)
    half = krt.shape[0]
    xt = jnp.concatenate([krt * gt[:half], krt * gt[half:]], axis=0).astype(BF16)
    kx = _dot(qx, xt)
    return kq, kx, jnp.sum(krt * krt, axis=0, keepdims=True)


def _chunk_update(kq, kx, ssq_rope, latb, mask, m_scr, l_scr, acc_scr):
    nk = HEADS * QK_NOPE
    n = latb.shape[0]
    kn = kq[:nk]
    ssq = jnp.sum((kn * kn).reshape(HEADS, QK_NOPE, n), axis=1) + ssq_rope
    rinv = lax.rsqrt(ssq * (1.0 / QK_HEAD) + EPS)
    nrow = kx.shape[0]
    s = (kq[nk:] + kx) * jnp.concatenate([rinv] * (nrow // HEADS), axis=0)
    if mask is not None:
        s = jnp.where(mask, s, NEG_INF)
    m_prev = m_scr[...]
    m_new = jnp.maximum(m_prev, jnp.max(s, axis=1, keepdims=True))
    alpha = jnp.exp2(m_prev - m_new)
    p = jnp.exp2(s - m_new)
    l_scr[...] = alpha * l_scr[...] + jnp.sum(p, axis=1, keepdims=True)
    acc_scr[...] = alpha * acc_scr[...] + _dot(p.astype(BF16), latb)
    m_scr[...] = m_new


def _sample_attn_kernel(pt_ref, qa_ref, qx_ref, wukt_ref, lat_hbm, krt_hbm, gt_ref, latn_ref, krn_ref, gtn_ref, o_ref,
                        lat_buf, krt_buf, lat_sem, krt_sem, lhs_scr, m_scr, l_scr, acc_scr,
                        *, layer, pages, chunk, n_new):
    j = pl.program_id(1)
    nj = pl.num_programs(1)
    step = pl.program_id(0) * nj + j
    slot = step % 2
    nk = wukt_ref.shape[0]
    nrow = qa_ref.shape[0]

    def page_copies(step_, slot_):
        copies = []
        for k in range(pages):
            page = pt_ref[step_ * pages + k]
            copies.append(pltpu.make_async_copy(lat_hbm.at[layer, page],
                                                lat_buf.at[slot_, pl.ds(k * PAGE, PAGE), :], lat_sem.at[slot_]))
            copies.append(pltpu.make_async_copy(krt_hbm.at[layer, page],
                                                krt_buf.at[slot_, :, pl.ds(k * PAGE, PAGE)], krt_sem.at[slot_]))
        return copies

    @pl.when(step == 0)
    def _():
        for cp in page_copies(0, 0):
            cp.start()

    @pl.when(step + 1 < pl.num_programs(0) * nj)
    def _():
        for cp in page_copies(step + 1, 1 - slot):
            cp.start()

    @pl.when(j == 0)
    def _():
        lhs_scr[0:nk, :] = wukt_ref[...]
        lhs_scr[nk:nk + nrow, :] = qa_ref[...]
        m_scr[...] = jnp.full_like(m_scr, NEG_INF)
        l_scr[...] = jnp.zeros_like(l_scr)
        acc_scr[...] = jnp.zeros_like(acc_scr)

    for cp in page_copies(step, slot):
        cp.wait()
    lat_cur = lat_buf.at[slot]
    kr_cur = krt_buf.at[slot]
    lhs = lhs_scr[...]
    qx = qx_ref[...]

    def stage_a(c):
        keys = slice(c * chunk, (c + 1) * chunk)
        latb = lat_cur[keys, :].astype(BF16)
        return _chunk_scores(lhs, qx, latb, kr_cur[:, keys], gt_ref[:, keys]) + (latb,)

    n_chunks = pages * PAGE // chunk
    nxt = stage_a(0)
    for c in range(n_chunks):
        cur = nxt
        if c + 1 < n_chunks:
            nxt = stage_a(c + 1)
        _chunk_update(*cur, None, m_scr, l_scr, acc_scr)

    @pl.when(j == pl.num_programs(1) - 1)
    def _():
        t_row = lax.broadcasted_iota(jnp.int32, (nrow, PAGE), 0) // HEADS
        kk = lax.broadcasted_iota(jnp.int32, (nrow, PAGE), 1)
        mask = (kk < n_new) & (kk <= t_row)
        latb = latn_ref[...].astype(BF16)
        _chunk_update(*_chunk_scores(lhs, qx, latb, krn_ref[...], gtn_ref[...]), latb, mask, m_scr, l_scr, acc_scr)
        o_ref[...] = acc_scr[...] / l_scr[...]


def _sample_attn_call(page_table, qabs, qx, wuk_t, cache_latent, cache_krope_t, layer, gt, lat_new, krt_new, gt_new,
                      n_new, pages):
    nb, n_pages = page_table.shape
    nj = n_pages // pages
    nrow = qabs.shape[1]
    chunk = min(pages * PAGE, SAMPLE_CHUNK)
    pt_flat = page_table.reshape(-1)
    keys = pages * PAGE
    grid_spec = pltpu.PrefetchScalarGridSpec(
        num_scalar_prefetch=1,
        grid=(nb, nj),
        in_specs=[
            pl.BlockSpec((None, nrow, KV_LORA), lambda b, j, pt: (b, 0, 0)),
            pl.BlockSpec((None,) + qx.shape[1:], lambda b, j, pt: (b, 0, 0)),
            pl.BlockSpec(wuk_t.shape, lambda b, j, pt: (0, 0)),
            pl.BlockSpec(memory_space=pl.ANY),
            pl.BlockSpec(memory_space=pl.ANY),
            pl.BlockSpec((2 * QK_ROPE, keys), lambda b, j, pt: (0, j)),
            pl.BlockSpec((None, PAGE, KV_LORA), lambda b, j, pt: (b, 0, 0)),
            pl.BlockSpec((None, QK_ROPE, PAGE), lambda b, j, pt: (b, 0, 0)),
            pl.BlockSpec((2 * QK_ROPE, PAGE), lambda b, j, pt: (0, 0)),
        ],
        out_specs=pl.BlockSpec((None, nrow, KV_LORA), lambda b, j, pt: (b, 0, 0)),
        scratch_shapes=[pltpu.VMEM((2, keys, KV_LORA), F32),
                        pltpu.VMEM((2, QK_ROPE, keys), F32),
                        pltpu.SemaphoreType.DMA((2,)),
                        pltpu.SemaphoreType.DMA((2,)),
                        pltpu.VMEM((wuk_t.shape[0] + nrow, KV_LORA), BF16),
                        pltpu.VMEM((nrow, 1), F32), pltpu.VMEM((nrow, 1), F32),
                        pltpu.VMEM((nrow, KV_LORA), F32)],
    )
    return pl.pallas_call(
        functools.partial(_sample_attn_kernel, layer=layer, pages=pages, chunk=chunk, n_new=n_new),
        grid_spec=grid_spec,
        out_shape=jax.ShapeDtypeStruct((nb, nrow, KV_LORA), F32),
        compiler_params=_params("arbitrary", "arbitrary"),
        name="sample_attn",
    )(pt_flat, qabs, qx, wuk_t, cache_latent, cache_krope_t, gt, lat_new, krt_new, gt_new)


def _attn_out_kernel(x_ref, a_ref, mod_ref, *rest, pre_proj):
    if pre_proj:
        wuv_ref, wo_ref, o_ref = rest
        a = _dot(a_ref[...].astype(BF16), wuv_ref[...]).astype(BF16)
    else:
        wo_ref, o_ref = rest
        a = a_ref[...].astype(BF16)
    o_ref[...] = x_ref[...] + mod_ref[0, 2] * _dot(a, wo_ref[...])


def _attn_out_call(x, a, mod, w_o, tiles_per_group, tm, wuv_bd=None):
    n, d = x.shape
    ws = [w_o] if wuv_bd is None else [wuv_bd, w_o]
    return pl.pallas_call(
        functools.partial(_attn_out_kernel, pre_proj=wuv_bd is not None),
        grid=(n // tm,),
        in_specs=[pl.BlockSpec((tm, d), lambda i: (i, 0)), pl.BlockSpec((tm, a.shape[1]), lambda i: (i, 0)),
                  _mod_spec(mod, tiles_per_group)] + [pl.BlockSpec(w.shape, lambda i: (0, 0)) for w in ws],
        out_specs=pl.BlockSpec((tm, d), lambda i: (i, 0)),
        out_shape=jax.ShapeDtypeStruct((n, d), F32),
        compiler_params=_params("arbitrary"),
        name="attn_out",
    )(x, a, mod, *ws)


def kernel(x_prompt, x_sample, c_prompt, c_sample, cache_latent, cache_krope, state_ssm_re, state_ssm_im, page_table, w_ada, b_ada, norm_g, ffn_w_in, ffn_w_out, ssm_w_in, ssm_a_re, ssm_a_im, ssm_log_dt, ssm_b_re, ssm_b_im, ssm_c_re, ssm_c_im, ssm_d, ssm_w_glu, mla_w_a, mla_g_qa, mla_g_kva, mla_w_uq, mla_w_ukv, mla_g_q, mla_g_k, mla_w_o):
    bsz, seq, d = x_prompt.shape
    nb, nt, _ = x_sample.shape
    depth = w_ada.shape[0]
    n_past = page_table.shape[1] * PAGE

    tm_p = _tile(seq, TOKEN_TILE)
    tpg_p = seq // tm_p
    n_s = nb * nt
    s5_t = _tile(seq, S5_CHUNK)
    s5_bs = _tile(nb, S5_SEQS)
    tq = _tile(seq, FLASH_TILE)
    pages = _tile(page_table.shape[1], PAGES_PER_STEP)

    c_all = jnp.concatenate([c_prompt, c_sample], axis=0)
    pad = (-c_all.shape[0]) % SUBLANES
    ada = _ada_call(jnp.pad(c_all, ((0, pad), (0, 0))), w_ada, b_ada)
    ada = ada.reshape(depth, -1, N_ADA, d)

    xp = x_prompt.reshape(bsz * seq, d)
    xs = x_sample.transpose(1, 0, 2).reshape(n_s, d)

    pos_p = jnp.arange(seq, dtype=jnp.int32)
    pos_s = n_past + jnp.arange(nt, dtype=jnp.int32)
    outs = {k: [] for k in ("lat_p", "kr_p", "lat_s", "kr_s", "hre_p", "him_p", "hre_s", "him_s")}
    w_in = ffn_w_in.astype(BF16)
    w_out = ffn_w_out.astype(BF16)

    for i in range(depth):
        ada_p = ada[i, :bsz]
        ada_s = ada[i, bsz:bsz + nb]
        mod_p = [ada_p[:, 3 * j:3 * j + 3, None, :] for j in range(3)]
        mod_s_seq = [ada_s[:, 3 * j:3 * j + 3].transpose(1, 0, 2) for j in range(3)]
        mod_s = [jnp.tile(m, (1, nt, 1))[None] for m in mod_s_seq]
        g = norm_g[i].reshape(3, 1, d)

        xp = _ffn_call(xp, g[0], mod_p[0], w_in, w_out, (i, 0), tpg_p, tm_p)
        xs = _ffn_call(xs, g[0], mod_s[0], w_in, w_out, (i, 0), 1, n_s)

        j = i // 2
        if i % 2 == 0:
            sw = _s5_weights(ssm_a_re[j], ssm_a_im[j], ssm_log_dt[j], ssm_b_re[j], ssm_b_im[j],
                             ssm_c_re[j], ssm_c_im[j], ssm_d[j])
            w_sin = ssm_w_in[j].astype(BF16)
            w_glu = ssm_w_glu[j].astype(BF16)
            xp, hre, him = _s5_prompt_call(xp, g[1], mod_p[1], w_sin, sw, w_glu, bsz, s5_t)
            groups, state = ssm_a_re[j].shape
            outs["hre_p"].append(hre.reshape(bsz, groups, state))
            outs["him_p"].append(him.reshape(bsz, groups, state))
            mod_blk = mod_s_seq[1].reshape(3, nb // s5_bs, s5_bs, d).transpose(1, 0, 2, 3)
            xs3, hre, him = _s5_sample_call(xs.reshape(nt, nb, d), g[1], mod_blk, w_sin, sw, w_glu,
                                            state_ssm_re[j].reshape(nb, -1), state_ssm_im[j].reshape(nb, -1), s5_bs)
            xs = xs3.reshape(n_s, d)
            outs["hre_s"].append(hre.reshape(nb, groups, state))
            outs["him_s"].append(him.reshape(nb, groups, state))
        else:
            mw = _mla_weights(mla_w_a[j], mla_w_uq[j], mla_w_ukv[j])
            w_o = mla_w_o[j].astype(BF16)
            g_q, g_k = mla_g_q[j], mla_g_k[j]
            tabs_p = _rope_tables(pos_p, g_q, ATTN_SCALE * LOG2E, False) + _rope_tables(pos_p, g_k, 1.0, False)
            q, k, v, ckv, kr = _mla_proj_call(xp, g[1], mod_p[1], mw, mla_g_qa[j], mla_g_kva[j], tabs_p, tpg_p, tm_p,
                                              True, BF16)
            outs["lat_p"].append(ckv.reshape(bsz, seq, KV_LORA))
            outs["kr_p"].append(kr.reshape(bsz, seq, QK_ROPE))
            attn = _flash_call(q, k, v, bsz, tq)
            xp = _attn_out_call(xp, attn, mod_p[1], w_o, tpg_p, tm_p)
            pos_rows = jnp.repeat(pos_s, nb)
            tabs_s = _rope_tables(pos_rows, g_q, ATTN_SCALE * LOG2E, True) + _rope_tables(pos_rows, g_k, 1.0, False)
            q_s, _, _, ckv_s, kr_s = _mla_proj_call(xs, g[1], mod_s[1], mw, mla_g_qa[j], mla_g_kva[j], tabs_s, 1, n_s,
                                                    False, F32)
            lat_new = ckv_s.reshape(nt, nb, KV_LORA).transpose(1, 0, 2)
            kr_new = kr_s.reshape(nt, nb, QK_ROPE).transpose(1, 0, 2)
            outs["lat_s"].append(lat_new)
            outs["kr_s"].append(kr_new)
            q_sm = q_s.reshape(nt, nb, HEADS * HEAD_PAD).transpose(1, 0, 2).reshape(n_s, HEADS * HEAD_PAD)
            gk_slab = jnp.concatenate([g_k[:QK_NOPE], jnp.zeros((HEAD_PAD - QK_NOPE,), F32)]).reshape(1, HEAD_PAD)
            qabs = _absorb_call(q_sm, gk_slab, mw["wabs"]).reshape(nb, nt * HEADS, KV_LORA)
            qx = q_sm.reshape(nb, nt * HEADS, HEAD_PAD)[..., QK_NOPE:].astype(BF16)
            pos_k = jnp.arange(n_past + PAGE, dtype=jnp.int32)
            ck, sk = _rope_cos_sin(pos_k)
            g1, g2 = g_k[QK_NOPE:QK_NOPE + 16], g_k[QK_NOPE + 16:]
            gt = jnp.concatenate([ck * g1, ck * g2, sk * g1, sk * g2], axis=1).T
            padn = ((0, 0), (0, PAGE - nt), (0, 0))
            krt_new = jnp.pad(kr_new, padn).transpose(0, 2, 1)
            ctx = _sample_attn_call(page_table, qabs, qx, mw["wuk_t"], cache_latent, cache_krope.transpose(0, 1, 3, 2),
                                    j, gt[:, :n_past], jnp.pad(lat_new, padn), krt_new, gt[:, n_past:], nt, pages)
            ctx_tm = ctx.reshape(nb, nt, HEADS * KV_LORA).transpose(1, 0, 2).reshape(n_s, HEADS * KV_LORA)
            xs = _attn_out_call(xs, ctx_tm, mod_s[1], w_o, 1, n_s, wuv_bd=mw["wuv_bd"])

        xp = _ffn_call(xp, g[2], mod_p[2], w_in, w_out, (i, 1), tpg_p, tm_p)
        xs = _ffn_call(xs, g[2], mod_s[2], w_in, w_out, (i, 1), 1, n_s)

    y_prompt = xp.reshape(bsz, seq, d)
    y_sample = xs.reshape(nt, nb, d).transpose(1, 0, 2)
    st = lambda k: jnp.stack(outs[k])
    return (y_prompt, y_sample, st("lat_p"), st("kr_p"), st("lat_s"), st("kr_s"),
            st("hre_p"), st("him_p"), st("hre_s"), st("him_s"))
```

```python
import functools
import math

import jax
import jax.numpy as jnp
from jax import lax
from jax.experimental import pallas as pl
from jax.experimental.pallas import tpu as pltpu

F32 = jnp.float32
BF16 = jnp.bfloat16

EPS = 1e-6
NEG_INF = -1e30
N_ADA = 9
PAGE = 128
SSM_GROUP = 16
HEADS = 16
Q_LORA = 384
KV_LORA = 256
QK_NOPE = 64
QK_ROPE = 32
QK_HEAD = QK_NOPE + QK_ROPE
V_HEAD = 64
HEAD_PAD = 128
ROPE_THETA = 10000.0
ATTN_SCALE = 1.0 / math.sqrt(QK_HEAD)
LOG2E = math.log2(math.e)
DT_MIN = 1e-3
DT_MAX = 1e-1

LANES = 128
SUBLANES = 8
VMEM_LIMIT = 56 * 1024 * 1024

TOKEN_TILE = 512
FFN_CHUNK = 256
S5_CHUNK = 256
S5_SEQS = 64
FLASH_TILE = 1024
FLASH_HEADS = 4
FLASH_ROWS = 256
PAGES_PER_STEP = 64
SAMPLE_CHUNK = 512


def _params(*sem):
    return pltpu.CompilerParams(dimension_semantics=sem, vmem_limit_bytes=VMEM_LIMIT)


def _dot(a, b):
    return jnp.dot(a, b, preferred_element_type=F32)


def _dot_nt(a, b):
    return lax.dot_general(a, b, (((1,), (1,)), ((), ())), preferred_element_type=F32)


def _mod_norm(x, g, scale, shift):
    ms = jnp.mean(x * x, axis=-1, keepdims=True)
    y = x * lax.rsqrt(ms + EPS) * g
    return y * (1.0 + scale) + shift


def _ada_kernel(c_ref, w_ref, b_ref, o_ref):
    c = c_ref[...]
    h = (c * jax.nn.sigmoid(c)).astype(BF16)
    o_ref[0] = _dot(h, w_ref[0].astype(BF16)) + b_ref[0]


def _ada_call(c_all, w_ada, b_ada):
    depth, d, n = w_ada.shape
    rows = c_all.shape[0]
    tn = 1152 if n % 1152 == 0 else n
    return pl.pallas_call(
        _ada_kernel,
        grid=(depth, n // tn),
        in_specs=[
            pl.BlockSpec((rows, d), lambda l, j: (0, 0)),
            pl.BlockSpec((1, d, tn), lambda l, j: (l, 0, j)),
            pl.BlockSpec((1, 1, tn), lambda l, j: (l, 0, j)),
        ],
        out_specs=pl.BlockSpec((1, rows, tn), lambda l, j: (l, 0, j)),
        out_shape=jax.ShapeDtypeStruct((depth, rows, n), F32),
        compiler_params=_params("arbitrary", "arbitrary"),
        name="ada",
    )(c_all, w_ada, b_ada.reshape(depth, 1, n))


def _mod_spec(mod, tiles_per_group):
    _, three, rows, d = mod.shape
    return pl.BlockSpec((1, three, rows, d), lambda i, *_: (i // tiles_per_group, 0, 0, 0))


def _tile(n, pref):
    return pref if n % pref == 0 else n


def _ffn_kernel(x_ref, g_ref, mod_ref, win_ref, wout_ref, o_ref, *, fc):
    x = x_ref[...]
    h = _mod_norm(x, g_ref[...], mod_ref[0, 1], mod_ref[0, 0]).astype(BF16)
    f = wout_ref.shape[0]

    def gate_up(c):
        return (_dot(h, win_ref[:, c * fc:(c + 1) * fc]), _dot(h, win_ref[:, f + c * fc:f + (c + 1) * fc]))

    acc = None
    nxt = gate_up(0)
    for c in range(f // fc):
        a, u = nxt
        if (c + 1) * fc < f:
            nxt = gate_up(c + 1)
        act = (a * jax.nn.sigmoid(a) * u).astype(BF16)
        part = _dot(act, wout_ref[c * fc:(c + 1) * fc, :])
        acc = part if acc is None else acc + part
    o_ref[...] = x + (0.5 * mod_ref[0, 2]) * acc


def _ffn_call(x, g, mod, w_in, w_out, which, tiles_per_group, tm):
    n, d = x.shape
    f = w_out.shape[2]
    fc = FFN_CHUNK if f % FFN_CHUNK == 0 else f
    layer, k = which
    return pl.pallas_call(
        functools.partial(_ffn_kernel, fc=fc),
        grid=(n // tm,),
        in_specs=[
            pl.BlockSpec((tm, d), lambda i: (i, 0)),
            pl.BlockSpec((1, d), lambda i: (0, 0)),
            _mod_spec(mod, tiles_per_group),
            pl.BlockSpec((None, None, d, 2 * f), lambda i: (layer, k, 0, 0)),
            pl.BlockSpec((None, None, f, d), lambda i: (layer, k, 0, 0)),
        ],
        out_specs=pl.BlockSpec((tm, d), lambda i: (i, 0)),
        out_shape=jax.ShapeDtypeStruct((n, d), F32),
        compiler_params=_params("arbitrary"),
        name="ffn",
    )(x, g, mod, w_in, w_out)


def _s5_discretize(a_re, a_im, log_dt, b_re, b_im):
    dt = jnp.exp(log_dt)[:, None]
    mag = jnp.exp(a_re * dt)
    ab_re, ab_im = mag * jnp.cos(a_im * dt), mag * jnp.sin(a_im * dt)
    den = a_re * a_re + a_im * a_im
    n_re, n_im = ab_re - 1.0, ab_im
    w_re = (n_re * a_re + n_im * a_im) / den
    w_im = (n_im * a_re - n_re * a_im) / den
    bb_re = w_re[..., None] * b_re - w_im[..., None] * b_im
    bb_im = w_re[..., None] * b_im + w_im[..., None] * b_re
    return ab_re, ab_im, bb_re, bb_im


def _s5_weights(a_re, a_im, log_dt, b_re, b_im, c_re, c_im, d_skip):
    groups, state = a_re.shape
    ab_re, ab_im, bb_re, bb_im = _s5_discretize(a_re, a_im, log_dt, b_re, b_im)
    nkt = groups // 16
    eye16 = jnp.eye(16, dtype=F32)

    def bd(bb):
        t = bb.transpose(0, 2, 1).reshape(nkt, 16, SSM_GROUP, state)
        full = t[:, :, :, None, :] * eye16[None, :, None, :, None]
        return full.reshape(nkt, 16 * SSM_GROUP, 16 * state).astype(BF16)

    nj = groups // 8
    sel = (jnp.arange(16)[None, None, :] ==
           (jnp.arange(8)[None, :, None] + 8 * (jnp.arange(nj)[:, None, None] % 2))).astype(F32)

    def cd(cc):
        t = cc.transpose(0, 2, 1).reshape(nj, 8, state, SSM_GROUP)
        full = t[:, :, :, None, :] * sel[:, :, None, :, None]
        return full.reshape(nj, 8 * state, 16 * SSM_GROUP).astype(BF16)

    return dict(bd_re=bd(bb_re), bd_im=bd(bb_im), cd_re=cd(c_re), cd_im=cd(c_im),
                a_re=ab_re.reshape(1, groups * state), a_im=ab_im.reshape(1, groups * state),
                dd=d_skip.reshape(1, -1))


def _s5_tail(x, u, y_parts, dd, gate, wglu):
    y = jnp.concatenate(y_parts, axis=1) + dd * u
    z = _dot(jax.nn.gelu(y).astype(BF16), wglu)
    half = z.shape[1] // 2
    return x + gate * (z[:, :half] * jax.nn.sigmoid(z[:, half:]))


def _s5_prompt_kernel(x_ref, g_ref, mod_ref, win_ref, bdre_ref, bdim_ref, are_ref, aim_ref, cdre_ref, cdim_ref,
                      dd_ref, wglu_ref, o_ref, hre_ref, him_ref, sre_scr, sim_scr, cre_scr, cim_scr, *, t_len, pitch):
    @pl.when(pl.program_id(1) == 0)
    def _():
        cre_scr[...] = jnp.zeros_like(cre_scr)
        cim_scr[...] = jnp.zeros_like(cim_scr)

    x = x_ref[...]
    h = _mod_norm(x, g_ref[...], mod_ref[0, 1], mod_ref[0, 0]).astype(BF16)
    u = _dot(h, win_ref[...])
    ub = u.astype(BF16)
    nkt, kw, sw = bdre_ref.shape
    per_kt = sw // LANES
    nvreg = nkt * per_kt // SUBLANES
    for kt in range(nkt):
        ukt = ub[:, kt * kw:(kt + 1) * kw]
        bre = _dot(ukt, bdre_ref[kt])
        bim = _dot(ukt, bdim_ref[kt])
        for m in range(per_kt):
            r0 = (kt * per_kt + m) * pitch
            sre_scr[r0:r0 + t_len, :] = bre[:, m * LANES:(m + 1) * LANES]
            sim_scr[r0:r0 + t_len, :] = bim[:, m * LANES:(m + 1) * LANES]

    vrows = lambda k: slice(k * SUBLANES, (k + 1) * SUBLANES)
    a_re = [are_ref[vrows(k), :] for k in range(nvreg)]
    a_im = [aim_ref[vrows(k), :] for k in range(nvreg)]

    def step(t, carry):
        new = []
        for k in range(nvreg):
            sr, si = carry[2 * k], carry[2 * k + 1]
            rows = pl.ds(k * SUBLANES * pitch + t, SUBLANES, stride=pitch)
            nr = a_re[k] * sr - a_im[k] * si + sre_scr[rows, :]
            ni = a_re[k] * si + a_im[k] * sr + sim_scr[rows, :]
            sre_scr[rows, :] = nr
            sim_scr[rows, :] = ni
            new += [nr, ni]
        return tuple(new)

    init = []
    for k in range(nvreg):
        init += [cre_scr[vrows(k), :], cim_scr[vrows(k), :]]
    fin = lax.fori_loop(0, t_len, step, tuple(init), unroll=8)
    for k in range(nvreg):
        cre_scr[vrows(k), :] = fin[2 * k]
        cim_scr[vrows(k), :] = fin[2 * k + 1]
        hre_ref[0, vrows(k), :] = fin[2 * k]
        him_ref[0, vrows(k), :] = fin[2 * k + 1]

    y_parts = []
    for kt in range(nkt):
        slabs = lambda scr: jnp.concatenate(
            [scr[(kt * per_kt + m) * pitch:(kt * per_kt + m) * pitch + t_len, :] for m in range(per_kt)], axis=1)
        y_parts.append(_dot(slabs(sre_scr).astype(BF16), cdre_ref[kt]) - _dot(slabs(sim_scr).astype(BF16), cdim_ref[kt]))
    o_ref[...] = _s5_tail(x, u, y_parts, dd_ref[...], mod_ref[0, 2], wglu_ref[...])


def _s5_prompt_call(x, g, mod, w_in, sw, w_glu, bsz, t_len):
    n, d = x.shape
    seq = n // bsz
    nt = seq // t_len
    pitch = t_len + SUBLANES
    nstate = sw["a_re"].shape[1]
    nslab = nstate // LANES
    nkt = sw["bd_re"].shape[0]
    a_re = sw["a_re"].reshape(nslab, LANES)
    a_im = sw["a_im"].reshape(nslab, LANES)
    cd_re = sw["cd_re"].reshape(nkt, nstate // nkt, -1)
    cd_im = sw["cd_im"].reshape(nkt, nstate // nkt, -1)
    const = lambda *shape: pl.BlockSpec(shape, lambda b, t: (0,) * len(shape))
    out, hre, him = pl.pallas_call(
        functools.partial(_s5_prompt_kernel, t_len=t_len, pitch=pitch),
        grid=(bsz, nt),
        in_specs=[
            pl.BlockSpec((t_len, d), lambda b, t: (b * nt + t, 0)),
            const(1, d),
            pl.BlockSpec((1, 3, 1, d), lambda b, t: (b, 0, 0, 0)),
            const(*w_in.shape),
            const(*sw["bd_re"].shape), const(*sw["bd_im"].shape),
            const(nslab, LANES), const(nslab, LANES),
            const(*cd_re.shape), const(*cd_im.shape),
            const(1, d),
            const(*w_glu.shape),
        ],
        out_specs=[
            pl.BlockSpec((t_len, d), lambda b, t: (b * nt + t, 0)),
            pl.BlockSpec((1, nslab, LANES), lambda b, t: (b, 0, 0)),
            pl.BlockSpec((1, nslab, LANES), lambda b, t: (b, 0, 0)),
        ],
        out_shape=[jax.ShapeDtypeStruct((n, d), F32),
                   jax.ShapeDtypeStruct((bsz, nslab, LANES), F32),
                   jax.ShapeDtypeStruct((bsz, nslab, LANES), F32)],
        scratch_shapes=[pltpu.VMEM((nslab * pitch, LANES), F32), pltpu.VMEM((nslab * pitch, LANES), F32),
                        pltpu.VMEM((nslab, LANES), F32), pltpu.VMEM((nslab, LANES), F32)],
        compiler_params=_params("arbitrary", "arbitrary"),
        name="s5_prompt",
    )(x, g, mod, w_in, sw["bd_re"], sw["bd_im"], a_re, a_im, cd_re, cd_im, sw["dd"], w_glu)
    return out, hre, him


def _s5_sample_kernel(x_ref, g_ref, mod_ref, win_ref, bdre_ref, bdim_ref, are_ref, aim_ref, h0re_ref, h0im_ref,
                      cdre_ref, cdim_ref, dd_ref, wglu_ref, o_ref, hre_ref, him_ref, sre_scr, sim_scr):
    nt, bs, d = x_ref.shape
    x = x_ref[...].reshape(nt * bs, d)
    mods = [jnp.concatenate([mod_ref[0, k]] * nt, axis=0) for k in range(3)]
    h = _mod_norm(x, g_ref[...], mods[1], mods[0]).astype(BF16)
    u = _dot(h, win_ref[...])
    ub = u.astype(BF16)
    nkt = bdre_ref.shape[0]
    kw = bdre_ref.shape[1]
    sw = bdre_ref.shape[2]
    for kt in range(nkt):
        ukt = ub[:, kt * kw:(kt + 1) * kw]
        sre_scr[:, kt * sw:(kt + 1) * sw] = _dot(ukt, bdre_ref[kt])
        sim_scr[:, kt * sw:(kt + 1) * sw] = _dot(ukt, bdim_ref[kt])
    a_re = are_ref[...]
    a_im = aim_ref[...]
    sr = h0re_ref[...]
    si = h0im_ref[...]
    for t in range(nt):
        rows = slice(t * bs, (t + 1) * bs)
        nr = a_re * sr - a_im * si + sre_scr[rows, :]
        ni = a_re * si + a_im * sr + sim_scr[rows, :]
        sre_scr[rows, :] = nr
        sim_scr[rows, :] = ni
        sr, si = nr, ni
    hre_ref[...] = sr
    him_ref[...] = si
    y_parts = []
    for kt in range(nkt):
        cols = slice(kt * sw, (kt + 1) * sw)
        y_parts.append(_dot(sre_scr[:, cols].astype(BF16), cdre_ref[kt])
                       - _dot(sim_scr[:, cols].astype(BF16), cdim_ref[kt]))
    out = _s5_tail(x, u, y_parts, dd_ref[...], mods[2], wglu_ref[...])
    o_ref[...] = out.reshape(nt, bs, d)


def _s5_sample_call(x_tm, g, mod, w_in, sw, w_glu, h0_re, h0_im, bs):
    nt, nb, d = x_tm.shape
    nstate = sw["a_re"].shape[1]
    nkt = sw["bd_re"].shape[0]
    cd_re = sw["cd_re"].reshape(nkt, nstate // nkt, -1)
    cd_im = sw["cd_im"].reshape(nkt, nstate // nkt, -1)
    const = lambda *shape: pl.BlockSpec(shape, lambda i: (0,) * len(shape))
    return pl.pallas_call(
        _s5_sample_kernel,
        grid=(nb // bs,),
        in_specs=[
            pl.BlockSpec((nt, bs, d), lambda i: (0, i, 0)),
            const(1, d),
            pl.BlockSpec((1, 3, bs, d), lambda i: (i, 0, 0, 0)),
            const(*w_in.shape),
            const(*sw["bd_re"].shape), const(*sw["bd_im"].shape),
            const(1, nstate), const(1, nstate),
            pl.BlockSpec((bs, nstate), lambda i: (i, 0)),
            pl.BlockSpec((bs, nstate), lambda i: (i, 0)),
            const(*cd_re.shape), const(*cd_im.shape),
            const(1, d),
            const(*w_glu.shape),
        ],
        out_specs=[
            pl.BlockSpec((nt, bs, d), lambda i: (0, i, 0)),
            pl.BlockSpec((bs, nstate), lambda i: (i, 0)),
            pl.BlockSpec((bs, nstate), lambda i: (i, 0)),
        ],
        out_shape=[jax.ShapeDtypeStruct((nt, nb, d), F32),
                   jax.ShapeDtypeStruct((nb, nstate), F32),
                   jax.ShapeDtypeStruct((nb, nstate), F32)],
        scratch_shapes=[pltpu.VMEM((nt * bs, nstate), F32), pltpu.VMEM((nt * bs, nstate), F32)],
        compiler_params=_params("arbitrary"),
        name="s5_sample",
    )(x_tm, g, mod, w_in, sw["bd_re"], sw["bd_im"], sw["a_re"], sw["a_im"], h0_re, h0_im, cd_re, cd_im,
      sw["dd"], w_glu)


def _mla_weights(w_a, w_uq, w_ukv):
    d = w_a.shape[0]
    w_aq = w_a[:, :Q_LORA]
    w_akv = w_a[:, Q_LORA:Q_LORA + KV_LORA]
    w_akr = w_a[:, Q_LORA + KV_LORA:]
    kr1, kr2 = w_akr[:, :16], w_akr[:, 16:]
    z64, z32 = jnp.zeros((d, QK_NOPE), F32), jnp.zeros((d, 32), F32)
    wkr = jnp.concatenate([z64, kr1, kr2, z32], axis=1)
    wkr_sw = jnp.concatenate([z64, -kr2, kr1, z32], axis=1)
    wq = w_uq.reshape(Q_LORA, HEADS, QK_HEAD)
    qn, q1, q2 = wq[..., :QK_NOPE], wq[..., QK_NOPE:QK_NOPE + 16], wq[..., QK_NOPE + 16:]
    wuq = jnp.concatenate([qn, q1, q2, q2, q1], axis=-1).reshape(Q_LORA, HEADS * HEAD_PAD)
    wuq_sw = jnp.concatenate([jnp.zeros_like(qn), -q2, q1, q1, -q2], axis=-1).reshape(Q_LORA, HEADS * HEAD_PAD)
    wkv = w_ukv.reshape(KV_LORA, HEADS, QK_NOPE + V_HEAD)
    wk, wv = wkv[..., :QK_NOPE], wkv[..., QK_NOPE:]
    wuk = jnp.concatenate([wk, jnp.zeros_like(wk)], axis=-1).reshape(KV_LORA, HEADS * HEAD_PAD)
    wuv = wv.reshape(KV_LORA, HEADS * V_HEAD)
    wuk_t = wk.transpose(1, 2, 0)
    wabs = jnp.concatenate([wuk_t, jnp.zeros_like(wuk_t)], axis=1)
    eye_h = jnp.eye(HEADS, dtype=F32)
    wuv_bd = (wv.transpose(1, 0, 2)[:, :, None, :] * eye_h[:, None, :, None]).reshape(HEADS * KV_LORA, HEADS * V_HEAD)
    c = lambda w: w.astype(BF16)
    return dict(w_aq=c(w_aq), w_akv=c(w_akv), w_akr=c(w_akr), wkr=c(wkr), wkr_sw=c(wkr_sw), wuq=c(wuq),
                wuq_sw=c(wuq_sw), wuk=c(wuk), wuv=c(wuv), wuk_t=c(wuk_t.reshape(HEADS * QK_NOPE, KV_LORA)),
                wabs=c(wabs), wuv_bd=c(wuv_bd))


def _rope_cos_sin(pos):
    inv = 1.0 / (ROPE_THETA ** (jnp.arange(0, QK_ROPE, 2, dtype=F32) / QK_ROPE))
    ang = pos.astype(F32)[:, None] * inv[None, :]
    return jnp.cos(ang), jnp.sin(ang)


def _rope_tables(pos, g, scale, with_partner):
    c, s = _rope_cos_sin(pos)
    n = pos.shape[0]
    gn, g1, g2 = g[:QK_NOPE], g[QK_NOPE:QK_NOPE + 16], g[QK_NOPE + 16:]
    ones = jnp.ones((n, 1), F32)
    z16, z64 = jnp.zeros((n, 16), F32), jnp.zeros((n, QK_NOPE), F32)
    if with_partner:
        gc = jnp.concatenate([ones * gn, c * g1, c * g2, c * g2, -(c * g1)], axis=1)
        gs = jnp.concatenate([z64, s * g2, s * g1, s * g1, -(s * g2)], axis=1)
    else:
        gc = jnp.concatenate([ones * gn, c * g1, c * g2, z16, z16], axis=1)
        gs = jnp.concatenate([z64, s * g2, s * g1, z16, z16], axis=1)
    return gc * scale, gs * scale


def _mla_proj_kernel(x_ref, g_ref, mod_ref, waq_ref, wakv_ref, wakr_ref, wkr_ref, wkrsw_ref, gqa_ref, gkva_ref,
                     wuq_ref, wuqsw_ref, wuk_ref, wuv_ref, gcq_ref, gsq_ref, gck_ref, gsk_ref,
                     q_ref, k_ref, v_ref, ckv_ref, kr_ref, *, with_kv):
    x = x_ref[...]
    h = _mod_norm(x, g_ref[...], mod_ref[0, 1], mod_ref[0, 0]).astype(BF16)
    aq = _dot(h, waq_ref[...])
    akv = _dot(h, wakv_ref[...])
    kr_ref[...] = _dot(h, wakr_ref[...])
    cq = (aq * lax.rsqrt(jnp.mean(aq * aq, axis=-1, keepdims=True) + EPS) * gqa_ref[...]).astype(BF16)
    ckv = akv * lax.rsqrt(jnp.mean(akv * akv, axis=-1, keepdims=True) + EPS) * gkva_ref[...]
    ckv_ref[...] = ckv
    q_pre = _dot(cq, wuq_ref[...])
    q_sw = _dot(cq, wuqsw_ref[...])
    tm = x.shape[0]
    in_head = lax.broadcasted_iota(jnp.int32, (tm, HEAD_PAD), 1) < QK_HEAD
    gcq, gsq = gcq_ref[...], gsq_ref[...]
    for hd in range(HEADS):
        cols = slice(hd * HEAD_PAD, (hd + 1) * HEAD_PAD)
        qp = q_pre[:, cols]
        ssq = jnp.sum(jnp.where(in_head, qp * qp, 0.0), axis=-1, keepdims=True)
        rinv = lax.rsqrt(ssq * (1.0 / QK_HEAD) + EPS)
        q_ref[:, cols] = (rinv * (qp * gcq + q_sw[:, cols] * gsq)).astype(q_ref.dtype)
    if with_kv:
        ckv_b = ckv.astype(BF16)
        k_pre = _dot(ckv_b, wuk_ref[...])
        v_all = _dot(ckv_b, wuv_ref[...])
        pair_w = 2 * V_HEAD
        one_col = (lax.broadcasted_iota(jnp.int32, (tm, pair_w), 1) == 0).astype(v_ref.dtype)
        for hp in range(HEADS // 2):
            v_ref[:, hp * 2 * pair_w:hp * 2 * pair_w + pair_w] = v_all[:, hp * pair_w:(hp + 1) * pair_w].astype(v_ref.dtype)
            v_ref[:, hp * 2 * pair_w + pair_w:(hp + 1) * 2 * pair_w] = one_col
        kr_slab = _dot(h, wkr_ref[...])
        kr_sw = _dot(h, wkrsw_ref[...])
        gck = gck_ref[...]
        kr_rot = kr_sw * gsk_ref[...]
        for hd in range(HEADS):
            cols = slice(hd * HEAD_PAD, (hd + 1) * HEAD_PAD)
            kp = k_pre[:, cols] + kr_slab
            ssq = jnp.sum(kp * kp, axis=-1, keepdims=True)
            rinv = lax.rsqrt(ssq * (1.0 / QK_HEAD) + EPS)
            k_ref[:, cols] = (rinv * (kp * gck + kr_rot)).astype(k_ref.dtype)
    else:
        k_ref[...] = jnp.zeros_like(k_ref)
        v_ref[...] = jnp.zeros_like(v_ref)


def _mla_proj_call(x, g, mod, mw, g_qa, g_kva, tabs, tiles_per_group, tm, with_kv, q_dtype):
    n, d = x.shape
    gcq, gsq, gck, gsk = tabs
    ntab = gcq.shape[0] // tm
    const = lambda a: pl.BlockSpec(a.shape, lambda i: (0,) * a.ndim)
    tab = pl.BlockSpec((tm, HEAD_PAD), lambda i: (i % ntab, 0))
    row = lambda w: pl.BlockSpec((tm, w), lambda i: (i, 0))
    kv_rows = tm if with_kv else SUBLANES
    kv_n = n if with_kv else SUBLANES * (n // tm)
    kvrow = lambda w: pl.BlockSpec((kv_rows, w), lambda i: (i, 0))
    g_qa2, g_kva2 = g_qa.reshape(1, -1), g_kva.reshape(1, -1)
    ws = [mw["w_aq"], mw["w_akv"], mw["w_akr"], mw["wkr"], mw["wkr_sw"], g_qa2, g_kva2,
          mw["wuq"], mw["wuq_sw"], mw["wuk"], mw["wuv"]]
    return pl.pallas_call(
        functools.partial(_mla_proj_kernel, with_kv=with_kv),
        grid=(n // tm,),
        in_specs=[row(d), pl.BlockSpec((1, d), lambda i: (0, 0)), _mod_spec(mod, tiles_per_group)]
                 + [const(w) for w in ws] + [tab, tab, tab, tab],
        out_specs=[row(HEADS * HEAD_PAD), kvrow(HEADS * HEAD_PAD), kvrow(2 * HEADS * V_HEAD), row(KV_LORA),
                   row(QK_ROPE)],
        out_shape=[jax.ShapeDtypeStruct((n, HEADS * HEAD_PAD), q_dtype),
                   jax.ShapeDtypeStruct((kv_n, HEADS * HEAD_PAD), BF16),
                   jax.ShapeDtypeStruct((kv_n, 2 * HEADS * V_HEAD), BF16),
                   jax.ShapeDtypeStruct((n, KV_LORA), F32),
                   jax.ShapeDtypeStruct((n, QK_ROPE), F32)],
        compiler_params=_params("arbitrary"),
        name="mla_proj",
    )(x, g, mod, *ws, gcq, gsq, gck, gsk)


def _flash_kernel(qi_ref, ki_ref, q_ref, k_ref, v_ref, o_ref, m_scr, acc_scr):
    p_id = pl.program_id(2)
    qi, ki = qi_ref[p_id], ki_ref[p_id]
    tq, tk = q_ref.shape[0], k_ref.shape[0]

    @pl.when(ki == 0)
    def _():
        m_scr[...] = jnp.full_like(m_scr, NEG_INF)
        acc_scr[...] = jnp.zeros_like(acc_scr)

    rb = min(tq, FLASH_ROWS)
    n_heads = m_scr.shape[0]
    pair_w = 2 * V_HEAD

    def update(masked):
        units = [(i, hh) for i in range(tq // rb) for hh in range(n_heads)]

        def n_keys(i):
            return (i + 1) * rb if masked else tk

        def scores(i, hh):
            cols = slice(hh * HEAD_PAD, (hh + 1) * HEAD_PAD)
            return _dot_nt(q_ref[i * rb:(i + 1) * rb, cols], k_ref[0:n_keys(i), cols])

        def finish(i, hh, s):
            rows = slice(i * rb, (i + 1) * rb)
            nk = n_keys(i)
            if masked:
                visible = (lax.broadcasted_iota(jnp.int32, (rb, nk), 1)
                           <= i * rb + lax.broadcasted_iota(jnp.int32, (rb, nk), 0))
                s = jnp.where(visible, s, NEG_INF)
            m_prev = m_scr[hh, rows, :]
            m_new = jnp.maximum(m_prev, jnp.max(s, axis=1, keepdims=True))
            alpha = jnp.exp2(m_prev - m_new)
            p = jnp.exp2((s - m_new).astype(BF16))
            v_cols = slice((hh // 2) * 2 * pair_w, (hh // 2 + 1) * 2 * pair_w)
            acc_scr[hh, rows, :] = alpha * acc_scr[hh, rows, :] + _dot(p, v_ref[0:nk, v_cols])
            m_scr[hh, rows, :] = m_new

        s_next = scores(*units[0])
        for idx, unit in enumerate(units):
            s_cur = s_next
            if idx + 1 < len(units):
                s_next = scores(*units[idx + 1])
            finish(*unit, s_cur)

    @pl.when(ki < qi)
    def _():
        update(False)

    @pl.when(ki == qi)
    def _():
        update(True)

    @pl.when(ki == qi)
    def _():
        lane = lax.broadcasted_iota(jnp.int32, (tq, pair_w), 1)
        for pair in range(n_heads // 2):
            out = [acc_scr[hh, :, 0:pair_w] / acc_scr[hh, :, pair_w:pair_w + 1] for hh in (2 * pair, 2 * pair + 1)]
            o_ref[:, pair * pair_w:(pair + 1) * pair_w] = jnp.where(lane < V_HEAD, out[0], out[1]).astype(o_ref.dtype)


def _flash_call(q, k, v, bsz, tq):
    n = q.shape[0]
    seq = n // bsz
    nq = seq // tq
    nh = FLASH_HEADS
    pairs = [(a, b) for a in range(nq) for b in range(a + 1)]
    qi_tab = jnp.asarray([p[0] for p in pairs], jnp.int32)
    ki_tab = jnp.asarray([p[1] for p in pairs], jnp.int32)
    grid_spec = pltpu.PrefetchScalarGridSpec(
        num_scalar_prefetch=2,
        grid=(bsz, HEADS // nh, len(pairs)),
        in_specs=[
            pl.BlockSpec((tq, nh * HEAD_PAD), lambda b, hp, p, qi, ki: (b * nq + qi[p], hp)),
            pl.BlockSpec((tq, nh * HEAD_PAD), lambda b, hp, p, qi, ki: (b * nq + ki[p], hp)),
            pl.BlockSpec((tq, nh * 2 * V_HEAD), lambda b, hp, p, qi, ki: (b * nq + ki[p], hp)),
        ],
        out_specs=pl.BlockSpec((tq, nh * V_HEAD), lambda b, hp, p, qi, ki: (b * nq + qi[p], hp)),
        scratch_shapes=[pltpu.VMEM((nh, tq, 1), F32), pltpu.VMEM((nh, tq, 4 * V_HEAD), F32)],
    )
    return pl.pallas_call(
        _flash_kernel,
        grid_spec=grid_spec,
        out_shape=jax.ShapeDtypeStruct((n, HEADS * V_HEAD), BF16),
        compiler_params=_params("arbitrary", "arbitrary", "arbitrary"),
        name="flash_prompt",
    )(qi_tab, ki_tab, q, k, v)


def _absorb_kernel(q_ref, gk_ref, wabs_ref, o_ref):
    gk = gk_ref[...]
    for hd in range(HEADS):
        qn = (q_ref[:, hd * HEAD_PAD:(hd + 1) * HEAD_PAD] * gk).astype(BF16)
        o_ref[:, hd * KV_LORA:(hd + 1) * KV_LORA] = _dot(qn, wabs_ref[hd]).astype(o_ref.dtype)


def _absorb_call(q, gk_slab, wabs):
    n = q.shape[0]
    return pl.pallas_call(
        _absorb_kernel,
        grid=(1,),
        in_specs=[pl.BlockSpec(q.shape, lambda i: (0, 0)), pl.BlockSpec(gk_slab.shape, lambda i: (0, 0)),
                  pl.BlockSpec(wabs.shape, lambda i: (0, 0, 0))],
        out_specs=pl.BlockSpec((n, HEADS * KV_LORA), lambda i: (0, 0)),
        out_shape=jax.ShapeDtypeStruct((n, HEADS * KV_LORA), BF16),
        compiler_params=_params("arbitrary"),
        name="absorb_q",
    )(q, gk_slab, wabs)


def _chunk_scores(lhs, qx, latb, krt, gt):
    kq = _dot_nt(lhs, latb)
    half = krt.shape[0]
    xt = jnp.concatenate([krt * gt[:half], krt * gt[half:]], axis=0).astype(BF16)
    kx = _dot(qx, xt)
    return kq, kx, jnp.sum(krt * krt, axis=0, keepdims=True)


def _chunk_update(kq, kx, ssq_rope, latb, mask, m_scr, l_scr, acc_scr):
    nk = HEADS * QK_NOPE
    n = latb.shape[0]
    kn = kq[:nk]
    ssq = jnp.sum((kn * kn).reshape(HEADS, QK_NOPE, n), axis=1) + ssq_rope
    rinv = lax.rsqrt(ssq * (1.0 / QK_HEAD) + EPS)
    nrow = kx.shape[0]
    s = (kq[nk:] + kx) * jnp.concatenate([rinv] * (nrow // HEADS), axis=0)
    if mask is not None:
        s = jnp.where(mask, s, NEG_INF)
    m_prev = m_scr[...]
    m_new = jnp.maximum(m_prev, jnp.max(s, axis=1, keepdims=True))
    alpha = jnp.exp2(m_prev - m_new)
    p = jnp.exp2(s - m_new)
    l_scr[...] = alpha * l_scr[...] + jnp.sum(p, axis=1, keepdims=True)
    acc_scr[...] = alpha * acc_scr[...] + _dot(p.astype(BF16), latb)
    m_scr[...] = m_new


def _sample_attn_kernel(pt_ref, qa_ref, qx_ref, wukt_ref, lat_hbm, krt_hbm, gt_ref, latn_ref, krn_ref, gtn_ref, o_ref,
                        lat_buf, krt_buf, lat_sem, krt_sem, lhs_scr, m_scr, l_scr, acc_scr,
                        *, layer, pages, chunk, n_new):
    j = pl.program_id(1)
    nj = pl.num_programs(1)
    step = pl.program_id(0) * nj + j
    slot = step % 2
    nk = wukt_ref.shape[0]
    nrow = qa_ref.shape[0]

    def page_copies(step_, slot_, which=range(pages)):
        copies = []
        for k in which:
            page = pt_ref[step_ * pages + k]
            copies.append(pltpu.make_async_copy(lat_hbm.at[layer, page],
                                                lat_buf.at[slot_, pl.ds(k * PAGE, PAGE), :], lat_sem.at[slot_]))
            copies.append(pltpu.make_async_copy(krt_hbm.at[layer, page],
                                                krt_buf.at[slot_, :, pl.ds(k * PAGE, PAGE)], krt_sem.at[slot_]))
        return copies

    @pl.when(step == 0)
    def _():
        for cp in page_copies(0, 0):
            cp.start()

    last_step = pl.num_programs(0) * nj - 1
    next_step = jnp.minimum(step + 1, last_step)

    @pl.when(j == 0)
    def _():
        lhs_scr[0:nk, :] = wukt_ref[...]
        lhs_scr[nk:nk + nrow, :] = qa_ref[...]
        m_scr[...] = jnp.full_like(m_scr, NEG_INF)
        l_scr[...] = jnp.zeros_like(l_scr)
        acc_scr[...] = jnp.zeros_like(acc_scr)

    for cp in page_copies(step, slot):
        cp.wait()
    lat_cur = lat_buf.at[slot]
    kr_cur = krt_buf.at[slot]
    lhs = lhs_scr[...]
    qx = qx_ref[...]

    def stage_a(c):
        keys = slice(c * chunk, (c + 1) * chunk)
        latb = lat_cur[keys, :].astype(BF16)
        return _chunk_scores(lhs, qx, latb, kr_cur[:, keys], gt_ref[:, keys]) + (latb,)

    n_chunks = pages * PAGE // chunk
    pages_per_chunk = pages // n_chunks
    nxt = stage_a(0)
    for c in range(n_chunks):
        cur = nxt
        if c + 1 < n_chunks:
            nxt = stage_a(c + 1)
        for cp in page_copies(next_step, 1 - slot, range(c * pages_per_chunk, (c + 1) * pages_per_chunk)):
            cp.start()
        _chunk_update(*cur, None, m_scr, l_scr, acc_scr)

    @pl.when(step == last_step)
    def _():
        for cp in page_copies(next_step, 1 - slot):
            cp.wait()

    @pl.when(j == pl.num_programs(1) - 1)
    def _():
        t_row = lax.broadcasted_iota(jnp.int32, (nrow, PAGE), 0) // HEADS
        kk = lax.broadcasted_iota(jnp.int32, (nrow, PAGE), 1)
        mask = (kk < n_new) & (kk <= t_row)
        latb = latn_ref[...].astype(BF16)
        _chunk_update(*_chunk_scores(lhs, qx, latb, krn_ref[...], gtn_ref[...]), latb, mask, m_scr, l_scr, acc_scr)
        o_ref[...] = acc_scr[...] / l_scr[...]


def _sample_attn_call(page_table, qabs, qx, wuk_t, cache_latent, cache_krope_t, layer, gt, lat_new, krt_new, gt_new,
                      n_new, pages):
    nb, n_pages = page_table.shape
    nj = n_pages // pages
    nrow = qabs.shape[1]
    chunk = min(pages * PAGE, SAMPLE_CHUNK)
    pt_flat = page_table.reshape(-1)
    keys = pages * PAGE
    grid_spec = pltpu.PrefetchScalarGridSpec(
        num_scalar_prefetch=1,
        grid=(nb, nj),
        in_specs=[
            pl.BlockSpec((None, nrow, KV_LORA), lambda b, j, pt: (b, 0, 0)),
            pl.BlockSpec((None,) + qx.shape[1:], lambda b, j, pt: (b, 0, 0)),
            pl.BlockSpec(wuk_t.shape, lambda b, j, pt: (0, 0)),
            pl.BlockSpec(memory_space=pl.ANY),
            pl.BlockSpec(memory_space=pl.ANY),
            pl.BlockSpec((2 * QK_ROPE, keys), lambda b, j, pt: (0, j)),
            pl.BlockSpec((None, PAGE, KV_LORA), lambda b, j, pt: (b, 0, 0)),
            pl.BlockSpec((None, QK_ROPE, PAGE), lambda b, j, pt: (b, 0, 0)),
            pl.BlockSpec((2 * QK_ROPE, PAGE), lambda b, j, pt: (0, 0)),
        ],
        out_specs=pl.BlockSpec((None, nrow, KV_LORA), lambda b, j, pt: (b, 0, 0)),
        scratch_shapes=[pltpu.VMEM((2, keys, KV_LORA), F32),
                        pltpu.VMEM((2, QK_ROPE, keys), F32),
                        pltpu.SemaphoreType.DMA((2,)),
                        pltpu.SemaphoreType.DMA((2,)),
                        pltpu.VMEM((wuk_t.shape[0] + nrow, KV_LORA), BF16),
                        pltpu.VMEM((nrow, 1), F32), pltpu.VMEM((nrow, 1), F32),
                        pltpu.VMEM((nrow, KV_LORA), F32)],
    )
    return pl.pallas_call(
        functools.partial(_sample_attn_kernel, layer=layer, pages=pages, chunk=chunk, n_new=n_new),
        grid_spec=grid_spec,
        out_shape=jax.ShapeDtypeStruct((nb, nrow, KV_LORA), F32),
        compiler_params=_params("arbitrary", "arbitrary"),
        name="sample_attn",
    )(pt_flat, qabs, qx, wuk_t, cache_latent, cache_krope_t, gt, lat_new, krt_new, gt_new)


def _attn_out_kernel(x_ref, a_ref, mod_ref, *rest, pre_proj):
    if pre_proj:
        wuv_ref, wo_ref, o_ref = rest
        a = _dot(a_ref[...].astype(BF16), wuv_ref[...]).astype(BF16)
    else:
        wo_ref, o_ref = rest
        a = a_ref[...].astype(BF16)
    o_ref[...] = x_ref[...] + mod_ref[0, 2] * _dot(a, wo_ref[...])


def _attn_out_call(x, a, mod, w_o, tiles_per_group, tm, wuv_bd=None):
    n, d = x.shape
    ws = [w_o] if wuv_bd is None else [wuv_bd, w_o]
    return pl.pallas_call(
        functools.partial(_attn_out_kernel, pre_proj=wuv_bd is not None),
        grid=(n // tm,),
        in_specs=[pl.BlockSpec((tm, d), lambda i: (i, 0)), pl.BlockSpec((tm, a.shape[1]), lambda i: (i, 0)),
                  _mod_spec(mod, tiles_per_group)] + [pl.BlockSpec(w.shape, lambda i: (0, 0)) for w in ws],
        out_specs=pl.BlockSpec((tm, d), lambda i: (i, 0)),
        out_shape=jax.ShapeDtypeStruct((n, d), F32),
        compiler_params=_params("arbitrary"),
        name="attn_out",
    )(x, a, mod, *ws)


def kernel(x_prompt, x_sample, c_prompt, c_sample, cache_latent, cache_krope, state_ssm_re, state_ssm_im, page_table, w_ada, b_ada, norm_g, ffn_w_in, ffn_w_out, ssm_w_in, ssm_a_re, ssm_a_im, ssm_log_dt, ssm_b_re, ssm_b_im, ssm_c_re, ssm_c_im, ssm_d, ssm_w_glu, mla_w_a, mla_g_qa, mla_g_kva, mla_w_uq, mla_w_ukv, mla_g_q, mla_g_k, mla_w_o):
    bsz, seq, d = x_prompt.shape
    nb, nt, _ = x_sample.shape
    depth = w_ada.shape[0]
    n_past = page_table.shape[1] * PAGE

    tm_p = _tile(seq, TOKEN_TILE)
    tpg_p = seq // tm_p
    n_s = nb * nt
    s5_t = _tile(seq, S5_CHUNK)
    s5_bs = _tile(nb, S5_SEQS)
    tq = _tile(seq, FLASH_TILE)
    pages = _tile(page_table.shape[1], PAGES_PER_STEP)

    c_all = jnp.concatenate([c_prompt, c_sample], axis=0)
    pad = (-c_all.shape[0]) % SUBLANES
    ada = _ada_call(jnp.pad(c_all, ((0, pad), (0, 0))), w_ada, b_ada)
    ada = ada.reshape(depth, -1, N_ADA, d)

    xp = x_prompt.reshape(bsz * seq, d)
    xs = x_sample.transpose(1, 0, 2).reshape(n_s, d)

    pos_p = jnp.arange(seq, dtype=jnp.int32)
    pos_s = n_past + jnp.arange(nt, dtype=jnp.int32)
    outs = {k: [] for k in ("lat_p", "kr_p", "lat_s", "kr_s", "hre_p", "him_p", "hre_s", "him_s")}
    w_in = ffn_w_in.astype(BF16)
    w_out = ffn_w_out.astype(BF16)

    for i in range(depth):
        ada_p = ada[i, :bsz]
        ada_s = ada[i, bsz:bsz + nb]
        mod_p = [ada_p[:, 3 * j:3 * j + 3, None, :] for j in range(3)]
        mod_s_seq = [ada_s[:, 3 * j:3 * j + 3].transpose(1, 0, 2) for j in range(3)]
        mod_s = [jnp.tile(m, (1, nt, 1))[None] for m in mod_s_seq]
        g = norm_g[i].reshape(3, 1, d)

        xp = _ffn_call(xp, g[0], mod_p[0], w_in, w_out, (i, 0), tpg_p, tm_p)
        xs = _ffn_call(xs, g[0], mod_s[0], w_in, w_out, (i, 0), 1, n_s)

        j = i // 2
        if i % 2 == 0:
            sw = _s5_weights(ssm_a_re[j], ssm_a_im[j], ssm_log_dt[j], ssm_b_re[j], ssm_b_im[j],
                             ssm_c_re[j], ssm_c_im[j], ssm_d[j])
            w_sin = ssm_w_in[j].astype(BF16)
            w_glu = ssm_w_glu[j].astype(BF16)
            xp, hre, him = _s5_prompt_call(xp, g[1], mod_p[1], w_sin, sw, w_glu, bsz, s5_t)
            groups, state = ssm_a_re[j].shape
            outs["hre_p"].append(hre.reshape(bsz, groups, state))
            outs["him_p"].append(him.reshape(bsz, groups, state))
            mod_blk = mod_s_seq[1].reshape(3, nb // s5_bs, s5_bs, d).transpose(1, 0, 2, 3)
            xs3, hre, him = _s5_sample_call(xs.reshape(nt, nb, d), g[1], mod_blk, w_sin, sw, w_glu,
                                            state_ssm_re[j].reshape(nb, -1), state_ssm_im[j].reshape(nb, -1), s5_bs)
            xs = xs3.reshape(n_s, d)
            outs["hre_s"].append(hre.reshape(nb, groups, state))
            outs["him_s"].append(him.reshape(nb, groups, state))
        else:
            mw = _mla_weights(mla_w_a[j], mla_w_uq[j], mla_w_ukv[j])
            w_o = mla_w_o[j].astype(BF16)
            g_q, g_k = mla_g_q[j], mla_g_k[j]
            tabs_p = _rope_tables(pos_p, g_q, ATTN_SCALE * LOG2E, False) + _rope_tables(pos_p, g_k, 1.0, False)
            q, k, v, ckv, kr = _mla_proj_call(xp, g[1], mod_p[1], mw, mla_g_qa[j], mla_g_kva[j], tabs_p, tpg_p, tm_p,
                                              True, BF16)
            outs["lat_p"].append(ckv.reshape(bsz, seq, KV_LORA))
            outs["kr_p"].append(kr.reshape(bsz, seq, QK_ROPE))
            attn = _flash_call(q, k, v, bsz, tq)
            xp = _attn_out_call(xp, attn, mod_p[1], w_o, tpg_p, tm_p)
            pos_rows = jnp.repeat(pos_s, nb)
            tabs_s = _rope_tables(pos_rows, g_q, ATTN_SCALE * LOG2E, True) + _rope_tables(pos_rows, g_k, 1.0, False)
            q_s, _, _, ckv_s, kr_s = _mla_proj_call(xs, g[1], mod_s[1], mw, mla_g_qa[j], mla_g_kva[j], tabs_s, 1, n_s,
                                                    False, F32)
            lat_new = ckv_s.reshape(nt, nb, KV_LORA).transpose(1, 0, 2)
            kr_new = kr_s.reshape(nt, nb, QK_ROPE).transpose(1, 0, 2)
            outs["lat_s"].append(lat_new)
            outs["kr_s"].append(kr_new)
            q_sm = q_s.reshape(nt, nb, HEADS * HEAD_PAD).transpose(1, 0, 2).reshape(n_s, HEADS * HEAD_PAD)
            gk_slab = jnp.concatenate([g_k[:QK_NOPE], jnp.zeros((HEAD_PAD - QK_NOPE,), F32)]).reshape(1, HEAD_PAD)
            qabs = _absorb_call(q_sm, gk_slab, mw["wabs"]).reshape(nb, nt * HEADS, KV_LORA)
            qx = q_sm.reshape(nb, nt * HEADS, HEAD_PAD)[..., QK_NOPE:].astype(BF16)
            pos_k = jnp.arange(n_past + PAGE, dtype=jnp.int32)
            ck, sk = _rope_cos_sin(pos_k)
            g1, g2 = g_k[QK_NOPE:QK_NOPE + 16], g_k[QK_NOPE + 16:]
            gt = jnp.concatenate([ck * g1, ck * g2, sk * g1, sk * g2], axis=1).T
            padn = ((0, 0), (0, PAGE - nt), (0, 0))
            krt_new = jnp.pad(kr_new, padn).transpose(0, 2, 1)
            ctx = _sample_attn_call(page_table, qabs, qx, mw["wuk_t"], cache_latent, cache_krope.transpose(0, 1, 3, 2),
                                    j, gt[:, :n_past], jnp.pad(lat_new, padn), krt_new, gt[:, n_past:], nt, pages)
            ctx_tm = ctx.reshape(nb, nt, HEADS * KV_LORA).transpose(1, 0, 2).reshape(n_s, HEADS * KV_LORA)
            xs = _attn_out_call(xs, ctx_tm, mod_s[1], w_o, 1, n_s, wuv_bd=mw["wuv_bd"])

        xp = _ffn_call(xp, g[2], mod_p[2], w_in, w_out, (i, 1), tpg_p, tm_p)
        xs = _ffn_call(xs, g[2], mod_s[2], w_in, w_out, (i, 1), 1, n_s)

    y_prompt = xp.reshape(bsz, seq, d)
    y_sample = xs.reshape(nt, nb, d).transpose(1, 0, 2)
    st = lambda k: jnp.stack(outs[k])
    return (y_prompt, y_sample, st("lat_p"), st("kr_p"), st("lat_s"), st("kr_s"),
            st("hre_p"), st("him_p"), st("hre_s"), st("him_s"))
```

```python
import functools
import math

import jax
import jax.numpy as jnp
from jax import lax
from jax.experimental import pallas as pl
from jax.experimental.pallas import tpu as pltpu

F32 = jnp.float32
BF16 = jnp.bfloat16

EPS = 1e-6
NEG_INF = -1e30
N_ADA = 9
PAGE = 128
SSM_GROUP = 16
HEADS = 16
Q_LORA = 384
KV_LORA = 256
QK_NOPE = 64
QK_ROPE = 32
QK_HEAD = QK_NOPE + QK_ROPE
V_HEAD = 64
HEAD_PAD = 128
ROPE_THETA = 10000.0
ATTN_SCALE = 1.0 / math.sqrt(QK_HEAD)
LOG2E = math.log2(math.e)
DT_MIN = 1e-3
DT_MAX = 1e-1

LANES = 128
SUBLANES = 8
VMEM_LIMIT = 56 * 1024 * 1024

TOKEN_TILE = 512
FFN_CHUNK = 256
S5_CHUNK = 256
S5_SEQS = 64
FLASH_TILE = 1024
FLASH_HEADS = 4
FLASH_ROWS = 256
PAGES_PER_STEP = 64
SAMPLE_CHUNK = 2048
SAMPLE_DEPTH = 1


def _params(*sem):
    return pltpu.CompilerParams(dimension_semantics=sem, vmem_limit_bytes=VMEM_LIMIT)


def _dot(a, b):
    return jnp.dot(a, b, preferred_element_type=F32)


def _dot_nt(a, b):
    return lax.dot_general(a, b, (((1,), (1,)), ((), ())), preferred_element_type=F32)


def _mod_norm(x, g, scale, shift):
    ms = jnp.mean(x * x, axis=-1, keepdims=True)
    y = x * lax.rsqrt(ms + EPS) * g
    return y * (1.0 + scale) + shift


def _ada_kernel(c_ref, w_ref, b_ref, o_ref):
    c = c_ref[...]
    h = (c * jax.nn.sigmoid(c)).astype(BF16)
    o_ref[0] = _dot(h, w_ref[0].astype(BF16)) + b_ref[0]


def _ada_call(c_all, w_ada, b_ada):
    depth, d, n = w_ada.shape
    rows = c_all.shape[0]
    tn = 1152 if n % 1152 == 0 else n
    return pl.pallas_call(
        _ada_kernel,
        grid=(depth, n // tn),
        in_specs=[
            pl.BlockSpec((rows, d), lambda l, j: (0, 0)),
            pl.BlockSpec((1, d, tn), lambda l, j: (l, 0, j)),
            pl.BlockSpec((1, 1, tn), lambda l, j: (l, 0, j)),
        ],
        out_specs=pl.BlockSpec((1, rows, tn), lambda l, j: (l, 0, j)),
        out_shape=jax.ShapeDtypeStruct((depth, rows, n), F32),
        compiler_params=_params("arbitrary", "arbitrary"),
        name="ada",
    )(c_all, w_ada, b_ada.reshape(depth, 1, n))


def _mod_spec(mod, tiles_per_group):
    _, three, rows, d = mod.shape
    return pl.BlockSpec((1, three, rows, d), lambda i, *_: (i // tiles_per_group, 0, 0, 0))


def _tile(n, pref):
    return pref if n % pref == 0 else n


def _ffn_kernel(x_ref, g_ref, mod_ref, win_ref, wout_ref, o_ref, *, fc):
    x = x_ref[...]
    h = _mod_norm(x, g_ref[...], mod_ref[0, 1], mod_ref[0, 0]).astype(BF16)
    f = wout_ref.shape[0]

    def gate_up(c):
        return (_dot(h, win_ref[:, c * fc:(c + 1) * fc]), _dot(h, win_ref[:, f + c * fc:f + (c + 1) * fc]))

    acc = None
    nxt = gate_up(0)
    for c in range(f // fc):
        a, u = nxt
        if (c + 1) * fc < f:
            nxt = gate_up(c + 1)
        act = (a * jax.nn.sigmoid(a) * u).astype(BF16)
        part = _dot(act, wout_ref[c * fc:(c + 1) * fc, :])
        acc = part if acc is None else acc + part
    o_ref[...] = x + (0.5 * mod_ref[0, 2]) * acc


def _ffn_call(x, g, mod, w_in, w_out, which, tiles_per_group, tm):
    n, d = x.shape
    f = w_out.shape[2]
    fc = FFN_CHUNK if f % FFN_CHUNK == 0 else f
    layer, k = which
    return pl.pallas_call(
        functools.partial(_ffn_kernel, fc=fc),
        grid=(n // tm,),
        in_specs=[
            pl.BlockSpec((tm, d), lambda i: (i, 0)),
            pl.BlockSpec((1, d), lambda i: (0, 0)),
            _mod_spec(mod, tiles_per_group),
            pl.BlockSpec((None, None, d, 2 * f), lambda i: (layer, k, 0, 0)),
            pl.BlockSpec((None, None, f, d), lambda i: (layer, k, 0, 0)),
        ],
        out_specs=pl.BlockSpec((tm, d), lambda i: (i, 0)),
        out_shape=jax.ShapeDtypeStruct((n, d), F32),
        compiler_params=_params("arbitrary"),
        name="ffn",
    )(x, g, mod, w_in, w_out)


def _s5_discretize(a_re, a_im, log_dt, b_re, b_im):
    dt = jnp.exp(log_dt)[:, None]
    mag = jnp.exp(a_re * dt)
    ab_re, ab_im = mag * jnp.cos(a_im * dt), mag * jnp.sin(a_im * dt)
    den = a_re * a_re + a_im * a_im
    n_re, n_im = ab_re - 1.0, ab_im
    w_re = (n_re * a_re + n_im * a_im) / den
    w_im = (n_im * a_re - n_re * a_im) / den
    bb_re = w_re[..., None] * b_re - w_im[..., None] * b_im
    bb_im = w_re[..., None] * b_im + w_im[..., None] * b_re
    return ab_re, ab_im, bb_re, bb_im


def _s5_weights(a_re, a_im, log_dt, b_re, b_im, c_re, c_im, d_skip):
    groups, state = a_re.shape
    ab_re, ab_im, bb_re, bb_im = _s5_discretize(a_re, a_im, log_dt, b_re, b_im)
    nkt = groups // 16
    eye16 = jnp.eye(16, dtype=F32)

    def bd(bb):
        t = bb.transpose(0, 2, 1).reshape(nkt, 16, SSM_GROUP, state)
        full = t[:, :, :, None, :] * eye16[None, :, None, :, None]
        return full.reshape(nkt, 16 * SSM_GROUP, 16 * state).astype(BF16)

    nj = groups // 8
    sel = (jnp.arange(16)[None, None, :] ==
           (jnp.arange(8)[None, :, None] + 8 * (jnp.arange(nj)[:, None, None] % 2))).astype(F32)

    def cd(cc):
        t = cc.transpose(0, 2, 1).reshape(nj, 8, state, SSM_GROUP)
        full = t[:, :, :, None, :] * sel[:, :, None, :, None]
        return full.reshape(nj, 8 * state, 16 * SSM_GROUP).astype(BF16)

    return dict(bd_re=bd(bb_re), bd_im=bd(bb_im), cd_re=cd(c_re), cd_im=cd(c_im),
                a_re=ab_re.reshape(1, groups * state), a_im=ab_im.reshape(1, groups * state),
                dd=d_skip.reshape(1, -1))


def _s5_tail(x, u, y_parts, dd, gate, wglu):
    y = jnp.concatenate(y_parts, axis=1) + dd * u
    z = _dot(jax.nn.gelu(y).astype(BF16), wglu)
    half = z.shape[1] // 2
    return x + gate * (z[:, :half] * jax.nn.sigmoid(z[:, half:]))


def _s5_prompt_kernel(x_ref, g_ref, mod_ref, win_ref, bdre_ref, bdim_ref, are_ref, aim_ref, cdre_ref, cdim_ref,
                      dd_ref, wglu_ref, o_ref, hre_ref, him_ref, sre_scr, sim_scr, cre_scr, cim_scr, *, t_len, pitch):
    @pl.when(pl.program_id(1) == 0)
    def _():
        cre_scr[...] = jnp.zeros_like(cre_scr)
        cim_scr[...] = jnp.zeros_like(cim_scr)

    x = x_ref[...]
    h = _mod_norm(x, g_ref[...], mod_ref[0, 1], mod_ref[0, 0]).astype(BF16)
    u = _dot(h, win_ref[...])
    ub = u.astype(BF16)
    nkt, kw, sw = bdre_ref.shape
    per_kt = sw // LANES
    nvreg = nkt * per_kt // SUBLANES
    for kt in range(nkt):
        ukt = ub[:, kt * kw:(kt + 1) * kw]
        bre = _dot(ukt, bdre_ref[kt])
        bim = _dot(ukt, bdim_ref[kt])
        for m in range(per_kt):
            r0 = (kt * per_kt + m) * pitch
            sre_scr[r0:r0 + t_len, :] = bre[:, m * LANES:(m + 1) * LANES]
            sim_scr[r0:r0 + t_len, :] = bim[:, m * LANES:(m + 1) * LANES]

    vrows = lambda k: slice(k * SUBLANES, (k + 1) * SUBLANES)
    a_re = [are_ref[vrows(k), :] for k in range(nvreg)]
    a_im = [aim_ref[vrows(k), :] for k in range(nvreg)]

    def step(t, carry):
        new = []
        for k in range(nvreg):
            sr, si = carry[2 * k], carry[2 * k + 1]
            rows = pl.ds(k * SUBLANES * pitch + t, SUBLANES, stride=pitch)
            nr = a_re[k] * sr - a_im[k] * si + sre_scr[rows, :]
            ni = a_re[k] * si + a_im[k] * sr + sim_scr[rows, :]
            sre_scr[rows, :] = nr
            sim_scr[rows, :] = ni
            new += [nr, ni]
        return tuple(new)

    init = []
    for k in range(nvreg):
        init += [cre_scr[vrows(k), :], cim_scr[vrows(k), :]]
    fin = lax.fori_loop(0, t_len, step, tuple(init), unroll=8)
    for k in range(nvreg):
        cre_scr[vrows(k), :] = fin[2 * k]
        cim_scr[vrows(k), :] = fin[2 * k + 1]
        hre_ref[0, vrows(k), :] = fin[2 * k]
        him_ref[0, vrows(k), :] = fin[2 * k + 1]

    y_parts = []
    for kt in range(nkt):
        slabs = lambda scr: jnp.concatenate(
            [scr[(kt * per_kt + m) * pitch:(kt * per_kt + m) * pitch + t_len, :] for m in range(per_kt)], axis=1)
        y_parts.append(_dot(slabs(sre_scr).astype(BF16), cdre_ref[kt]) - _dot(slabs(sim_scr).astype(BF16), cdim_ref[kt]))
    o_ref[...] = _s5_tail(x, u, y_parts, dd_ref[...], mod_ref[0, 2], wglu_ref[...])


def _s5_prompt_call(x, g, mod, w_in, sw, w_glu, bsz, t_len):
    n, d = x.shape
    seq = n // bsz
    nt = seq // t_len
    pitch = t_len + SUBLANES
    nstate = sw["a_re"].shape[1]
    nslab = nstate // LANES
    nkt = sw["bd_re"].shape[0]
    a_re = sw["a_re"].reshape(nslab, LANES)
    a_im = sw["a_im"].reshape(nslab, LANES)
    cd_re = sw["cd_re"].reshape(nkt, nstate // nkt, -1)
    cd_im = sw["cd_im"].reshape(nkt, nstate // nkt, -1)
    const = lambda *shape: pl.BlockSpec(shape, lambda b, t: (0,) * len(shape))
    out, hre, him = pl.pallas_call(
        functools.partial(_s5_prompt_kernel, t_len=t_len, pitch=pitch),
        grid=(bsz, nt),
        in_specs=[
            pl.BlockSpec((t_len, d), lambda b, t: (b * nt + t, 0)),
            const(1, d),
            pl.BlockSpec((1, 3, 1, d), lambda b, t: (b, 0, 0, 0)),
            const(*w_in.shape),
            const(*sw["bd_re"].shape), const(*sw["bd_im"].shape),
            const(nslab, LANES), const(nslab, LANES),
            const(*cd_re.shape), const(*cd_im.shape),
            const(1, d),
            const(*w_glu.shape),
        ],
        out_specs=[
            pl.BlockSpec((t_len, d), lambda b, t: (b * nt + t, 0)),
            pl.BlockSpec((1, nslab, LANES), lambda b, t: (b, 0, 0)),
            pl.BlockSpec((1, nslab, LANES), lambda b, t: (b, 0, 0)),
        ],
        out_shape=[jax.ShapeDtypeStruct((n, d), F32),
                   jax.ShapeDtypeStruct((bsz, nslab, LANES), F32),
                   jax.ShapeDtypeStruct((bsz, nslab, LANES), F32)],
        scratch_shapes=[pltpu.VMEM((nslab * pitch, LANES), F32), pltpu.VMEM((nslab * pitch, LANES), F32),
                        pltpu.VMEM((nslab, LANES), F32), pltpu.VMEM((nslab, LANES), F32)],
        compiler_params=_params("arbitrary", "arbitrary"),
        name="s5_prompt",
    )(x, g, mod, w_in, sw["bd_re"], sw["bd_im"], a_re, a_im, cd_re, cd_im, sw["dd"], w_glu)
    return out, hre, him


def _s5_sample_kernel(x_ref, g_ref, mod_ref, win_ref, bdre_ref, bdim_ref, are_ref, aim_ref, h0re_ref, h0im_ref,
                      cdre_ref, cdim_ref, dd_ref, wglu_ref, o_ref, hre_ref, him_ref, sre_scr, sim_scr):
    nt, bs, d = x_ref.shape
    x = x_ref[...].reshape(nt * bs, d)
    mods = [jnp.concatenate([mod_ref[0, k]] * nt, axis=0) for k in range(3)]
    h = _mod_norm(x, g_ref[...], mods[1], mods[0]).astype(BF16)
    u = _dot(h, win_ref[...])
    ub = u.astype(BF16)
    nkt = bdre_ref.shape[0]
    kw = bdre_ref.shape[1]
    sw = bdre_ref.shape[2]
    for kt in range(nkt):
        ukt = ub[:, kt * kw:(kt + 1) * kw]
        sre_scr[:, kt * sw:(kt + 1) * sw] = _dot(ukt, bdre_ref[kt])
        sim_scr[:, kt * sw:(kt + 1) * sw] = _dot(ukt, bdim_ref[kt])
    a_re = are_ref[...]
    a_im = aim_ref[...]
    sr = h0re_ref[...]
    si = h0im_ref[...]
    for t in range(nt):
        rows = slice(t * bs, (t + 1) * bs)
        nr = a_re * sr - a_im * si + sre_scr[rows, :]
        ni = a_re * si + a_im * sr + sim_scr[rows, :]
        sre_scr[rows, :] = nr
        sim_scr[rows, :] = ni
        sr, si = nr, ni
    hre_ref[...] = sr
    him_ref[...] = si
    y_parts = []
    for kt in range(nkt):
        cols = slice(kt * sw, (kt + 1) * sw)
        y_parts.append(_dot(sre_scr[:, cols].astype(BF16), cdre_ref[kt])
                       - _dot(sim_scr[:, cols].astype(BF16), cdim_ref[kt]))
    out = _s5_tail(x, u, y_parts, dd_ref[...], mods[2], wglu_ref[...])
    o_ref[...] = out.reshape(nt, bs, d)


def _s5_sample_call(x_tm, g, mod, w_in, sw, w_glu, h0_re, h0_im, bs):
    nt, nb, d = x_tm.shape
    nstate = sw["a_re"].shape[1]
    nkt = sw["bd_re"].shape[0]
    cd_re = sw["cd_re"].reshape(nkt, nstate // nkt, -1)
    cd_im = sw["cd_im"].reshape(nkt, nstate // nkt, -1)
    const = lambda *shape: pl.BlockSpec(shape, lambda i: (0,) * len(shape))
    return pl.pallas_call(
        _s5_sample_kernel,
        grid=(nb // bs,),
        in_specs=[
            pl.BlockSpec((nt, bs, d), lambda i: (0, i, 0)),
            const(1, d),
            pl.BlockSpec((1, 3, bs, d), lambda i: (i, 0, 0, 0)),
            const(*w_in.shape),
            const(*sw["bd_re"].shape), const(*sw["bd_im"].shape),
            const(1, nstate), const(1, nstate),
            pl.BlockSpec((bs, nstate), lambda i: (i, 0)),
            pl.BlockSpec((bs, nstate), lambda i: (i, 0)),
            const(*cd_re.shape), const(*cd_im.shape),
            const(1, d),
            const(*w_glu.shape),
        ],
        out_specs=[
            pl.BlockSpec((nt, bs, d), lambda i: (0, i, 0)),
            pl.BlockSpec((bs, nstate), lambda i: (i, 0)),
            pl.BlockSpec((bs, nstate), lambda i: (i, 0)),
        ],
        out_shape=[jax.ShapeDtypeStruct((nt, nb, d), F32),
                   jax.ShapeDtypeStruct((nb, nstate), F32),
                   jax.ShapeDtypeStruct((nb, nstate), F32)],
        scratch_shapes=[pltpu.VMEM((nt * bs, nstate), F32), pltpu.VMEM((nt * bs, nstate), F32)],
        compiler_params=_params("arbitrary"),
        name="s5_sample",
    )(x_tm, g, mod, w_in, sw["bd_re"], sw["bd_im"], sw["a_re"], sw["a_im"], h0_re, h0_im, cd_re, cd_im,
      sw["dd"], w_glu)


def _mla_weights(w_a, w_uq, w_ukv):
    d = w_a.shape[0]
    w_aq = w_a[:, :Q_LORA]
    w_akv = w_a[:, Q_LORA:Q_LORA + KV_LORA]
    w_akr = w_a[:, Q_LORA + KV_LORA:]
    kr1, kr2 = w_akr[:, :16], w_akr[:, 16:]
    z64, z32 = jnp.zeros((d, QK_NOPE), F32), jnp.zeros((d, 32), F32)
    wkr = jnp.concatenate([z64, kr1, kr2, z32], axis=1)
    wkr_sw = jnp.concatenate([z64, -kr2, kr1, z32], axis=1)
    wq = w_uq.reshape(Q_LORA, HEADS, QK_HEAD)
    qn, q1, q2 = wq[..., :QK_NOPE], wq[..., QK_NOPE:QK_NOPE + 16], wq[..., QK_NOPE + 16:]
    wuq = jnp.concatenate([qn, q1, q2, q2, q1], axis=-1).reshape(Q_LORA, HEADS * HEAD_PAD)
    wuq_sw = jnp.concatenate([jnp.zeros_like(qn), -q2, q1, q1, -q2], axis=-1).reshape(Q_LORA, HEADS * HEAD_PAD)
    wkv = w_ukv.reshape(KV_LORA, HEADS, QK_NOPE + V_HEAD)
    wk, wv = wkv[..., :QK_NOPE], wkv[..., QK_NOPE:]
    wuk = jnp.concatenate([wk, jnp.zeros_like(wk)], axis=-1).reshape(KV_LORA, HEADS * HEAD_PAD)
    wuv = wv.reshape(KV_LORA, HEADS * V_HEAD)
    wuk_t = wk.transpose(1, 2, 0)
    wabs = jnp.concatenate([wuk_t, jnp.zeros_like(wuk_t)], axis=1)
    eye_h = jnp.eye(HEADS, dtype=F32)
    wuv_bd = (wv.transpose(1, 0, 2)[:, :, None, :] * eye_h[:, None, :, None]).reshape(HEADS * KV_LORA, HEADS * V_HEAD)
    c = lambda w: w.astype(BF16)
    return dict(w_aq=c(w_aq), w_akv=c(w_akv), w_akr=c(w_akr), wkr=c(wkr), wkr_sw=c(wkr_sw), wuq=c(wuq),
                wuq_sw=c(wuq_sw), wuk=c(wuk), wuv=c(wuv),
                wuk_t=c(wk.transpose(2, 1, 0).reshape(QK_NOPE * HEADS, KV_LORA)),
                wabs=c(wabs), wuv_bd=c(wuv_bd))


def _rope_cos_sin(pos):
    inv = 1.0 / (ROPE_THETA ** (jnp.arange(0, QK_ROPE, 2, dtype=F32) / QK_ROPE))
    ang = pos.astype(F32)[:, None] * inv[None, :]
    return jnp.cos(ang), jnp.sin(ang)


def _rope_tables(pos, g, scale, with_partner):
    c, s = _rope_cos_sin(pos)
    n = pos.shape[0]
    gn, g1, g2 = g[:QK_NOPE], g[QK_NOPE:QK_NOPE + 16], g[QK_NOPE + 16:]
    ones = jnp.ones((n, 1), F32)
    z16, z64 = jnp.zeros((n, 16), F32), jnp.zeros((n, QK_NOPE), F32)
    if with_partner:
        gc = jnp.concatenate([ones * gn, c * g1, c * g2, c * g2, -(c * g1)], axis=1)
        gs = jnp.concatenate([z64, s * g2, s * g1, s * g1, -(s * g2)], axis=1)
    else:
        gc = jnp.concatenate([ones * gn, c * g1, c * g2, z16, z16], axis=1)
        gs = jnp.concatenate([z64, s * g2, s * g1, z16, z16], axis=1)
    return gc * scale, gs * scale


def _mla_proj_kernel(x_ref, g_ref, mod_ref, waq_ref, wakv_ref, wakr_ref, wkr_ref, wkrsw_ref, gqa_ref, gkva_ref,
                     wuq_ref, wuqsw_ref, wuk_ref, wuv_ref, gcq_ref, gsq_ref, gck_ref, gsk_ref,
                     q_ref, k_ref, v_ref, ckv_ref, kr_ref, *, with_kv):
    x = x_ref[...]
    h = _mod_norm(x, g_ref[...], mod_ref[0, 1], mod_ref[0, 0]).astype(BF16)
    aq = _dot(h, waq_ref[...])
    akv = _dot(h, wakv_ref[...])
    kr_ref[...] = _dot(h, wakr_ref[...])
    cq = (aq * lax.rsqrt(jnp.mean(aq * aq, axis=-1, keepdims=True) + EPS) * gqa_ref[...]).astype(BF16)
    ckv = akv * lax.rsqrt(jnp.mean(akv * akv, axis=-1, keepdims=True) + EPS) * gkva_ref[...]
    ckv_ref[...] = ckv
    q_pre = _dot(cq, wuq_ref[...])
    q_sw = _dot(cq, wuqsw_ref[...])
    tm = x.shape[0]
    in_head = lax.broadcasted_iota(jnp.int32, (tm, HEAD_PAD), 1) < QK_HEAD
    gcq, gsq = gcq_ref[...], gsq_ref[...]
    for hd in range(HEADS):
        cols = slice(hd * HEAD_PAD, (hd + 1) * HEAD_PAD)
        qp = q_pre[:, cols]
        ssq = jnp.sum(jnp.where(in_head, qp * qp, 0.0), axis=-1, keepdims=True)
        rinv = lax.rsqrt(ssq * (1.0 / QK_HEAD) + EPS)
        q_ref[:, cols] = (rinv * (qp * gcq + q_sw[:, cols] * gsq)).astype(q_ref.dtype)
    if with_kv:
        ckv_b = ckv.astype(BF16)
        k_pre = _dot(ckv_b, wuk_ref[...])
        v_all = _dot(ckv_b, wuv_ref[...])
        pair_w = 2 * V_HEAD
        one_col = (lax.broadcasted_iota(jnp.int32, (tm, pair_w), 1) == 0).astype(v_ref.dtype)
        for hp in range(HEADS // 2):
            v_ref[:, hp * 2 * pair_w:hp * 2 * pair_w + pair_w] = v_all[:, hp * pair_w:(hp + 1) * pair_w].astype(v_ref.dtype)
            v_ref[:, hp * 2 * pair_w + pair_w:(hp + 1) * 2 * pair_w] = one_col
        kr_slab = _dot(h, wkr_ref[...])
        kr_sw = _dot(h, wkrsw_ref[...])
        gck = gck_ref[...]
        kr_rot = kr_sw * gsk_ref[...]
        for hd in range(HEADS):
            cols = slice(hd * HEAD_PAD, (hd + 1) * HEAD_PAD)
            kp = k_pre[:, cols] + kr_slab
            ssq = jnp.sum(kp * kp, axis=-1, keepdims=True)
            rinv = lax.rsqrt(ssq * (1.0 / QK_HEAD) + EPS)
            k_ref[:, cols] = (rinv * (kp * gck + kr_rot)).astype(k_ref.dtype)
    else:
        k_ref[...] = jnp.zeros_like(k_ref)
        v_ref[...] = jnp.zeros_like(v_ref)


def _mla_proj_call(x, g, mod, mw, g_qa, g_kva, tabs, tiles_per_group, tm, with_kv, q_dtype):
    n, d = x.shape
    gcq, gsq, gck, gsk = tabs
    ntab = gcq.shape[0] // tm
    const = lambda a: pl.BlockSpec(a.shape, lambda i: (0,) * a.ndim)
    tab = pl.BlockSpec((tm, HEAD_PAD), lambda i: (i % ntab, 0))
    row = lambda w: pl.BlockSpec((tm, w), lambda i: (i, 0))
    kv_rows = tm if with_kv else SUBLANES
    kv_n = n if with_kv else SUBLANES * (n // tm)
    kvrow = lambda w: pl.BlockSpec((kv_rows, w), lambda i: (i, 0))
    g_qa2, g_kva2 = g_qa.reshape(1, -1), g_kva.reshape(1, -1)
    ws = [mw["w_aq"], mw["w_akv"], mw["w_akr"], mw["wkr"], mw["wkr_sw"], g_qa2, g_kva2,
          mw["wuq"], mw["wuq_sw"], mw["wuk"], mw["wuv"]]
    return pl.pallas_call(
        functools.partial(_mla_proj_kernel, with_kv=with_kv),
        grid=(n // tm,),
        in_specs=[row(d), pl.BlockSpec((1, d), lambda i: (0, 0)), _mod_spec(mod, tiles_per_group)]
                 + [const(w) for w in ws] + [tab, tab, tab, tab],
        out_specs=[row(HEADS * HEAD_PAD), kvrow(HEADS * HEAD_PAD), kvrow(2 * HEADS * V_HEAD), row(KV_LORA),
                   row(QK_ROPE)],
        out_shape=[jax.ShapeDtypeStruct((n, HEADS * HEAD_PAD), q_dtype),
                   jax.ShapeDtypeStruct((kv_n, HEADS * HEAD_PAD), BF16),
                   jax.ShapeDtypeStruct((kv_n, 2 * HEADS * V_HEAD), BF16),
                   jax.ShapeDtypeStruct((n, KV_LORA), F32),
                   jax.ShapeDtypeStruct((n, QK_ROPE), F32)],
        compiler_params=_params("arbitrary"),
        name="mla_proj",
    )(x, g, mod, *ws, gcq, gsq, gck, gsk)


def _flash_kernel(qi_ref, ki_ref, q_ref, k_ref, v_ref, o_ref, m_scr, acc_scr):
    p_id = pl.program_id(2)
    qi, ki = qi_ref[p_id], ki_ref[p_id]
    tq, tk = q_ref.shape[0], k_ref.shape[0]

    @pl.when(ki == 0)
    def _():
        m_scr[...] = jnp.full_like(m_scr, NEG_INF)
        acc_scr[...] = jnp.zeros_like(acc_scr)

    rb = min(tq, FLASH_ROWS)
    n_heads = m_scr.shape[0]
    pair_w = 2 * V_HEAD

    def update(masked):
        units = [(i, hh) for i in range(tq // rb) for hh in range(n_heads)]

        def n_keys(i):
            return (i + 1) * rb if masked else tk

        def scores(i, hh):
            cols = slice(hh * HEAD_PAD, (hh + 1) * HEAD_PAD)
            return _dot_nt(q_ref[i * rb:(i + 1) * rb, cols], k_ref[0:n_keys(i), cols])

        def finish(i, hh, s):
            rows = slice(i * rb, (i + 1) * rb)
            nk = n_keys(i)
            if masked:
                visible = (lax.broadcasted_iota(jnp.int32, (rb, nk), 1)
                           <= i * rb + lax.broadcasted_iota(jnp.int32, (rb, nk), 0))
                s = jnp.where(visible, s, NEG_INF)
            m_prev = m_scr[hh, rows, :]
            m_new = jnp.maximum(m_prev, jnp.max(s, axis=1, keepdims=True))
            alpha = jnp.exp2(m_prev - m_new)
            p = jnp.exp2((s - m_new).astype(BF16))
            v_cols = slice((hh // 2) * 2 * pair_w, (hh // 2 + 1) * 2 * pair_w)
            acc_scr[hh, rows, :] = alpha * acc_scr[hh, rows, :] + _dot(p, v_ref[0:nk, v_cols])
            m_scr[hh, rows, :] = m_new

        s_next = scores(*units[0])
        for idx, unit in enumerate(units):
            s_cur = s_next
            if idx + 1 < len(units):
                s_next = scores(*units[idx + 1])
            finish(*unit, s_cur)

    @pl.when(ki < qi)
    def _():
        update(False)

    @pl.when(ki == qi)
    def _():
        update(True)

    @pl.when(ki == qi)
    def _():
        lane = lax.broadcasted_iota(jnp.int32, (tq, pair_w), 1)
        for pair in range(n_heads // 2):
            out = [acc_scr[hh, :, 0:pair_w] / acc_scr[hh, :, pair_w:pair_w + 1] for hh in (2 * pair, 2 * pair + 1)]
            o_ref[:, pair * pair_w:(pair + 1) * pair_w] = jnp.where(lane < V_HEAD, out[0], out[1]).astype(o_ref.dtype)


def _flash_call(q, k, v, bsz, tq):
    n = q.shape[0]
    seq = n // bsz
    nq = seq // tq
    nh = FLASH_HEADS
    pairs = [(a, b) for a in range(nq) for b in range(a + 1)]
    qi_tab = jnp.asarray([p[0] for p in pairs], jnp.int32)
    ki_tab = jnp.asarray([p[1] for p in pairs], jnp.int32)
    grid_spec = pltpu.PrefetchScalarGridSpec(
        num_scalar_prefetch=2,
        grid=(bsz, HEADS // nh, len(pairs)),
        in_specs=[
            pl.BlockSpec((tq, nh * HEAD_PAD), lambda b, hp, p, qi, ki: (b * nq + qi[p], hp)),
            pl.BlockSpec((tq, nh * HEAD_PAD), lambda b, hp, p, qi, ki: (b * nq + ki[p], hp)),
            pl.BlockSpec((tq, nh * 2 * V_HEAD), lambda b, hp, p, qi, ki: (b * nq + ki[p], hp)),
        ],
        out_specs=pl.BlockSpec((tq, nh * V_HEAD), lambda b, hp, p, qi, ki: (b * nq + qi[p], hp)),
        scratch_shapes=[pltpu.VMEM((nh, tq, 1), F32), pltpu.VMEM((nh, tq, 4 * V_HEAD), F32)],
    )
    return pl.pallas_call(
        _flash_kernel,
        grid_spec=grid_spec,
        out_shape=jax.ShapeDtypeStruct((n, HEADS * V_HEAD), BF16),
        compiler_params=_params("arbitrary", "arbitrary", "arbitrary"),
        name="flash_prompt",
    )(qi_tab, ki_tab, q, k, v)


def _absorb_kernel(q_ref, gk_ref, wabs_ref, o_ref):
    gk = gk_ref[...]
    for hd in range(HEADS):
        qn = (q_ref[:, hd * HEAD_PAD:(hd + 1) * HEAD_PAD] * gk).astype(BF16)
        o_ref[:, hd * KV_LORA:(hd + 1) * KV_LORA] = _dot(qn, wabs_ref[hd]).astype(o_ref.dtype)


def _absorb_call(q, gk_slab, wabs):
    n = q.shape[0]
    return pl.pallas_call(
        _absorb_kernel,
        grid=(1,),
        in_specs=[pl.BlockSpec(q.shape, lambda i: (0, 0)), pl.BlockSpec(gk_slab.shape, lambda i: (0, 0)),
                  pl.BlockSpec(wabs.shape, lambda i: (0, 0, 0))],
        out_specs=pl.BlockSpec((n, HEADS * KV_LORA), lambda i: (0, 0)),
        out_shape=jax.ShapeDtypeStruct((n, HEADS * KV_LORA), BF16),
        compiler_params=_params("arbitrary"),
        name="absorb_q",
    )(q, gk_slab, wabs)


def _chunk_scores(lhs, qx, latb, krt, gt):
    kq = _dot_nt(lhs, latb)
    half = krt.shape[0]
    xt = jnp.concatenate([krt * gt[:half], krt * gt[half:]], axis=0).astype(BF16)
    kx = _dot(qx, xt)
    return kq, kx, jnp.sum(krt * krt, axis=0, keepdims=True)


def _chunk_update(kq, kx, ssq_rope, latb, mask, m_scr, l_scr, acc_scr):
    nk = HEADS * QK_NOPE
    n = latb.shape[0]
    kn = kq[:nk]
    ssq = jnp.sum((kn * kn).reshape(QK_NOPE, HEADS, n), axis=0) + ssq_rope
    rinv = lax.rsqrt(ssq * (1.0 / QK_HEAD) + EPS)
    nrow = kx.shape[0]
    s = (kq[nk:] + kx) * jnp.concatenate([rinv] * (nrow // HEADS), axis=0)
    if mask is not None:
        s = jnp.where(mask, s, NEG_INF)
    m_prev = m_scr[...]
    m_new = jnp.maximum(m_prev, jnp.max(s, axis=1, keepdims=True))
    alpha = jnp.exp2(m_prev - m_new)
    p = jnp.exp2(s - m_new)
    l_scr[...] = alpha * l_scr[...] + jnp.sum(p, axis=1, keepdims=True)
    acc_scr[...] = alpha * acc_scr[...] + _dot(p.astype(BF16), latb)
    m_scr[...] = m_new


def _sample_attn_kernel(pt_ref, qa_ref, qx_ref, wukt_ref, lat_hbm, krt_hbm, gt_ref, latn_ref, krn_ref, gtn_ref, o_ref,
                        lat_buf, krt_buf, lat_sem, krt_sem, lhs_scr, m_scr, l_scr, acc_scr,
                        *, layer, pages, chunk, n_new):
    j = pl.program_id(1)
    nj = pl.num_programs(1)
    step = pl.program_id(0) * nj + j
    slot = step % 2
    nk = wukt_ref.shape[0]
    nrow = qa_ref.shape[0]

    def page_copies(step_, slot_, which=range(pages)):
        copies = []
        for k in which:
            page = pt_ref[step_ * pages + k]
            copies.append(pltpu.make_async_copy(lat_hbm.at[layer, page],
                                                lat_buf.at[slot_, pl.ds(k * PAGE, PAGE), :], lat_sem.at[slot_]))
            copies.append(pltpu.make_async_copy(krt_hbm.at[layer, page],
                                                krt_buf.at[slot_, :, pl.ds(k * PAGE, PAGE)], krt_sem.at[slot_]))
        return copies

    @pl.when(step == 0)
    def _():
        for cp in page_copies(0, 0):
            cp.start()

    last_step = pl.num_programs(0) * nj - 1
    next_step = jnp.minimum(step + 1, last_step)

    @pl.when(j == 0)
    def _():
        lhs_scr[0:nk, :] = wukt_ref[...]
        lhs_scr[nk:nk + nrow, :] = qa_ref[...]
        m_scr[...] = jnp.full_like(m_scr, NEG_INF)
        l_scr[...] = jnp.zeros_like(l_scr)
        acc_scr[...] = jnp.zeros_like(acc_scr)

    for cp in page_copies(step, slot):
        cp.wait()
    lat_cur = lat_buf.at[slot]
    kr_cur = krt_buf.at[slot]
    lhs = lhs_scr[...]
    qx = qx_ref[...]

    def stage_a(c):
        keys = slice(c * chunk, (c + 1) * chunk)
        latb = lat_cur[keys, :].astype(BF16)
        return _chunk_scores(lhs, qx, latb, kr_cur[:, keys], gt_ref[:, keys]) + (latb,)

    def stage_new():
        latb = latn_ref[...].astype(BF16)
        return _chunk_scores(lhs, qx, latb, krn_ref[...], gtn_ref[...]) + (latb,)

    is_last = j == nj - 1
    t_row = lax.broadcasted_iota(jnp.int32, (nrow, PAGE), 0) // HEADS
    kk = lax.broadcasted_iota(jnp.int32, (nrow, PAGE), 1)
    new_mask = (kk < n_new) & (kk <= t_row) & is_last

    n_chunks = pages * PAGE // chunk
    pages_per_chunk = pages // n_chunks
    stages = [functools.partial(stage_a, c) for c in range(n_chunks)] + [stage_new]
    masks = [None] * n_chunks + [new_mask]
    depth = min(SAMPLE_DEPTH, len(stages))
    queue = [stages[c]() for c in range(depth)]
    for c in range(len(stages)):
        cur = queue.pop(0)
        if c + depth < len(stages):
            queue.append(stages[c + depth]())
        if c < n_chunks:
            for cp in page_copies(next_step, 1 - slot, range(c * pages_per_chunk, (c + 1) * pages_per_chunk)):
                cp.start()
        _chunk_update(*cur, masks[c], m_scr, l_scr, acc_scr)

    @pl.when(step == last_step)
    def _():
        for cp in page_copies(next_step, 1 - slot):
            cp.wait()

    @pl.when(is_last)
    def _():
        o_ref[...] = acc_scr[...] / l_scr[...]


def _sample_attn_call(page_table, qabs, qx, wuk_t, cache_latent, cache_krope_t, layer, gt, lat_new, krt_new, gt_new,
                      n_new, pages):
    nb, n_pages = page_table.shape
    nj = n_pages // pages
    nrow = qabs.shape[1]
    chunk = min(pages * PAGE, SAMPLE_CHUNK)
    pt_flat = page_table.reshape(-1)
    keys = pages * PAGE
    grid_spec = pltpu.PrefetchScalarGridSpec(
        num_scalar_prefetch=1,
        grid=(nb, nj),
        in_specs=[
            pl.BlockSpec((None, nrow, KV_LORA), lambda b, j, pt: (b, 0, 0)),
            pl.BlockSpec((None,) + qx.shape[1:], lambda b, j, pt: (b, 0, 0)),
            pl.BlockSpec(wuk_t.shape, lambda b, j, pt: (0, 0)),
            pl.BlockSpec(memory_space=pl.ANY),
            pl.BlockSpec(memory_space=pl.ANY),
            pl.BlockSpec((2 * QK_ROPE, keys), lambda b, j, pt: (0, j)),
            pl.BlockSpec((None, PAGE, KV_LORA), lambda b, j, pt: (b, 0, 0)),
            pl.BlockSpec((None, QK_ROPE, PAGE), lambda b, j, pt: (b, 0, 0)),
            pl.BlockSpec((2 * QK_ROPE, PAGE), lambda b, j, pt: (0, 0)),
        ],
        out_specs=pl.BlockSpec((None, nrow, KV_LORA), lambda b, j, pt: (b, 0, 0)),
        scratch_shapes=[pltpu.VMEM((2, keys, KV_LORA), F32),
                        pltpu.VMEM((2, QK_ROPE, keys), F32),
                        pltpu.SemaphoreType.DMA((2,)),
                        pltpu.SemaphoreType.DMA((2,)),
                        pltpu.VMEM((wuk_t.shape[0] + nrow, KV_LORA), BF16),
                        pltpu.VMEM((nrow, 1), F32), pltpu.VMEM((nrow, 1), F32),
                        pltpu.VMEM((nrow, KV_LORA), F32)],
    )
    return pl.pallas_call(
        functools.partial(_sample_attn_kernel, layer=layer, pages=pages, chunk=chunk, n_new=n_new),
        grid_spec=grid_spec,
        out_shape=jax.ShapeDtypeStruct((nb, nrow, KV_LORA), F32),
        compiler_params=_params("arbitrary", "arbitrary"),
        name="sample_attn",
    )(pt_flat, qabs, qx, wuk_t, cache_latent, cache_krope_t, gt, lat_new, krt_new, gt_new)


def _attn_out_kernel(x_ref, a_ref, mod_ref, *rest, pre_proj):
    if pre_proj:
        wuv_ref, wo_ref, o_ref = rest
        a = _dot(a_ref[...].astype(BF16), wuv_ref[...]).astype(BF16)
    else:
        wo_ref, o_ref = rest
        a = a_ref[...].astype(BF16)
    o_ref[...] = x_ref[...] + mod_ref[0, 2] * _dot(a, wo_ref[...])


def _attn_out_call(x, a, mod, w_o, tiles_per_group, tm, wuv_bd=None):
    n, d = x.shape
    ws = [w_o] if wuv_bd is None else [wuv_bd, w_o]
    return pl.pallas_call(
        functools.partial(_attn_out_kernel, pre_proj=wuv_bd is not None),
        grid=(n // tm,),
        in_specs=[pl.BlockSpec((tm, d), lambda i: (i, 0)), pl.BlockSpec((tm, a.shape[1]), lambda i: (i, 0)),
                  _mod_spec(mod, tiles_per_group)] + [pl.BlockSpec(w.shape, lambda i: (0, 0)) for w in ws],
        out_specs=pl.BlockSpec((tm, d), lambda i: (i, 0)),
        out_shape=jax.ShapeDtypeStruct((n, d), F32),
        compiler_params=_params("arbitrary"),
        name="attn_out",
    )(x, a, mod, *ws)


def kernel(x_prompt, x_sample, c_prompt, c_sample, cache_latent, cache_krope, state_ssm_re, state_ssm_im, page_table, w_ada, b_ada, norm_g, ffn_w_in, ffn_w_out, ssm_w_in, ssm_a_re, ssm_a_im, ssm_log_dt, ssm_b_re, ssm_b_im, ssm_c_re, ssm_c_im, ssm_d, ssm_w_glu, mla_w_a, mla_g_qa, mla_g_kva, mla_w_uq, mla_w_ukv, mla_g_q, mla_g_k, mla_w_o):
    bsz, seq, d = x_prompt.shape
    nb, nt, _ = x_sample.shape
    depth = w_ada.shape[0]
    n_past = page_table.shape[1] * PAGE

    tm_p = _tile(seq, TOKEN_TILE)
    tpg_p = seq // tm_p
    n_s = nb * nt
    s5_t = _tile(seq, S5_CHUNK)
    s5_bs = _tile(nb, S5_SEQS)
    tq = _tile(seq, FLASH_TILE)
    pages = _tile(page_table.shape[1], PAGES_PER_STEP)

    c_all = jnp.concatenate([c_prompt, c_sample], axis=0)
    pad = (-c_all.shape[0]) % SUBLANES
    ada = _ada_call(jnp.pad(c_all, ((0, pad), (0, 0))), w_ada, b_ada)
    ada = ada.reshape(depth, -1, N_ADA, d)

    xp = x_prompt.reshape(bsz * seq, d)
    xs = x_sample.transpose(1, 0, 2).reshape(n_s, d)

    pos_p = jnp.arange(seq, dtype=jnp.int32)
    pos_s = n_past + jnp.arange(nt, dtype=jnp.int32)
    outs = {k: [] for k in ("lat_p", "kr_p", "lat_s", "kr_s", "hre_p", "him_p", "hre_s", "him_s")}
    w_in = ffn_w_in.astype(BF16)
    w_out = ffn_w_out.astype(BF16)

    for i in range(depth):
        ada_p = ada[i, :bsz]
        ada_s = ada[i, bsz:bsz + nb]
        mod_p = [ada_p[:, 3 * j:3 * j + 3, None, :] for j in range(3)]
        mod_s_seq = [ada_s[:, 3 * j:3 * j + 3].transpose(1, 0, 2) for j in range(3)]
        mod_s = [jnp.tile(m, (1, nt, 1))[None] for m in mod_s_seq]
        g = norm_g[i].reshape(3, 1, d)

        xp = _ffn_call(xp, g[0], mod_p[0], w_in, w_out, (i, 0), tpg_p, tm_p)
        xs = _ffn_call(xs, g[0], mod_s[0], w_in, w_out, (i, 0), 1, n_s)

        j = i // 2
        if i % 2 == 0:
            sw = _s5_weights(ssm_a_re[j], ssm_a_im[j], ssm_log_dt[j], ssm_b_re[j], ssm_b_im[j],
                             ssm_c_re[j], ssm_c_im[j], ssm_d[j])
            w_sin = ssm_w_in[j].astype(BF16)
            w_glu = ssm_w_glu[j].astype(BF16)
            xp, hre, him = _s5_prompt_call(xp, g[1], mod_p[1], w_sin, sw, w_glu, bsz, s5_t)
            groups, state = ssm_a_re[j].shape
            outs["hre_p"].append(hre.reshape(bsz, groups, state))
            outs["him_p"].append(him.reshape(bsz, groups, state))
            mod_blk = mod_s_seq[1].reshape(3, nb // s5_bs, s5_bs, d).transpose(1, 0, 2, 3)
            xs3, hre, him = _s5_sample_call(xs.reshape(nt, nb, d), g[1], mod_blk, w_sin, sw, w_glu,
                                            state_ssm_re[j].reshape(nb, -1), state_ssm_im[j].reshape(nb, -1), s5_bs)
            xs = xs3.reshape(n_s, d)
            outs["hre_s"].append(hre.reshape(nb, groups, state))
            outs["him_s"].append(him.reshape(nb, groups, state))
        else:
            mw = _mla_weights(mla_w_a[j], mla_w_uq[j], mla_w_ukv[j])
            w_o = mla_w_o[j].astype(BF16)
            g_q, g_k = mla_g_q[j], mla_g_k[j]
            tabs_p = _rope_tables(pos_p, g_q, ATTN_SCALE * LOG2E, False) + _rope_tables(pos_p, g_k, 1.0, False)
            q, k, v, ckv, kr = _mla_proj_call(xp, g[1], mod_p[1], mw, mla_g_qa[j], mla_g_kva[j], tabs_p, tpg_p, tm_p,
                                              True, BF16)
            outs["lat_p"].append(ckv.reshape(bsz, seq, KV_LORA))
            outs["kr_p"].append(kr.reshape(bsz, seq, QK_ROPE))
            attn = _flash_call(q, k, v, bsz, tq)
            xp = _attn_out_call(xp, attn, mod_p[1], w_o, tpg_p, tm_p)
            pos_rows = jnp.repeat(pos_s, nb)
            tabs_s = _rope_tables(pos_rows, g_q, ATTN_SCALE * LOG2E, True) + _rope_tables(pos_rows, g_k, 1.0, False)
            q_s, _, _, ckv_s, kr_s = _mla_proj_call(xs, g[1], mod_s[1], mw, mla_g_qa[j], mla_g_kva[j], tabs_s, 1, n_s,
                                                    False, F32)
            lat_new = ckv_s.reshape(nt, nb, KV_LORA).transpose(1, 0, 2)
            kr_new = kr_s.reshape(nt, nb, QK_ROPE).transpose(1, 0, 2)
            outs["lat_s"].append(lat_new)
            outs["kr_s"].append(kr_new)
            q_sm = q_s.reshape(nt, nb, HEADS * HEAD_PAD).transpose(1, 0, 2).reshape(n_s, HEADS * HEAD_PAD)
            gk_slab = jnp.concatenate([g_k[:QK_NOPE], jnp.zeros((HEAD_PAD - QK_NOPE,), F32)]).reshape(1, HEAD_PAD)
            qabs = _absorb_call(q_sm, gk_slab, mw["wabs"]).reshape(nb, nt * HEADS, KV_LORA)
            qx = q_sm.reshape(nb, nt * HEADS, HEAD_PAD)[..., QK_NOPE:].astype(BF16)
            pos_k = jnp.arange(n_past + PAGE, dtype=jnp.int32)
            ck, sk = _rope_cos_sin(pos_k)
            g1, g2 = g_k[QK_NOPE:QK_NOPE + 16], g_k[QK_NOPE + 16:]
            gt = jnp.concatenate([ck * g1, ck * g2, sk * g1, sk * g2], axis=1).T
            padn = ((0, 0), (0, PAGE - nt), (0, 0))
            krt_new = jnp.pad(kr_new, padn).transpose(0, 2, 1)
            ctx = _sample_attn_call(page_table, qabs, qx, mw["wuk_t"], cache_latent, cache_krope.transpose(0, 1, 3, 2),
                                    j, gt[:, :n_past], jnp.pad(lat_new, padn), krt_new, gt[:, n_past:], nt, pages)
            ctx_tm = ctx.reshape(nb, nt, HEADS * KV_LORA).transpose(1, 0, 2).reshape(n_s, HEADS * KV_LORA)
            xs = _attn_out_call(xs, ctx_tm, mod_s[1], w_o, 1, n_s, wuv_bd=mw["wuv_bd"])

        xp = _ffn_call(xp, g[2], mod_p[2], w_in, w_out, (i, 1), tpg_p, tm_p)
        xs = _ffn_call(xs, g[2], mod_s[2], w_in, w_out, (i, 1), 1, n_s)

    y_prompt = xp.reshape(bsz, seq, d)
    y_sample = xs.reshape(nt, nb, d).transpose(1, 0, 2)
    st = lambda k: jnp.stack(outs[k])
    return (y_prompt, y_sample, st("lat_p"), st("kr_p"), st("lat_s"), st("kr_s"),
            st("hre_p"), st("him_p"), st("hre_s"), st("him_s"))
```

```python
import functools
import math

import jax
import jax.numpy as jnp
from jax import lax
from jax.experimental import pallas as pl
from jax.experimental.pallas import tpu as pltpu

F32 = jnp.float32
BF16 = jnp.bfloat16

EPS = 1e-6
NEG_INF = -1e30
N_ADA = 9
PAGE = 128
SSM_GROUP = 16
HEADS = 16
Q_LORA = 384
KV_LORA = 256
QK_NOPE = 64
QK_ROPE = 32
QK_HEAD = QK_NOPE + QK_ROPE
V_HEAD = 64
HEAD_PAD = 128
ROPE_THETA = 10000.0
ATTN_SCALE = 1.0 / math.sqrt(QK_HEAD)
LOG2E = math.log2(math.e)
DT_MIN = 1e-3
DT_MAX = 1e-1

LANES = 128
SUBLANES = 8
VMEM_LIMIT = 56 * 1024 * 1024

TOKEN_TILE = 512
FFN_CHUNK = 256
S5_CHUNK = 256
S5_SEQS = 64
FLASH_TILE = 1024
FLASH_HEADS = 4
FLASH_ROWS = 256
PAGES_PER_STEP = 64
SAMPLE_CHUNK = 1024
SAMPLE_DEPTH = 1


def _params(*sem):
    return pltpu.CompilerParams(dimension_semantics=sem, vmem_limit_bytes=VMEM_LIMIT)


def _dot(a, b):
    return jnp.dot(a, b, preferred_element_type=F32)


def _dot_nt(a, b):
    return lax.dot_general(a, b, (((1,), (1,)), ((), ())), preferred_element_type=F32)


def _mod_norm(x, g, scale, shift):
    ms = jnp.mean(x * x, axis=-1, keepdims=True)
    y = x * lax.rsqrt(ms + EPS) * g
    return y * (1.0 + scale) + shift


def _ada_kernel(c_ref, w_ref, b_ref, o_ref):
    c = c_ref[...]
    h = (c * jax.nn.sigmoid(c)).astype(BF16)
    o_ref[0, 0] = _dot(h, w_ref[0].astype(BF16)) + b_ref[0]


def _ada_call(c_all, w_ada, b_ada):
    depth, d, n = w_ada.shape
    rows = c_all.shape[0]
    return pl.pallas_call(
        _ada_kernel,
        grid=(depth, n // d),
        in_specs=[
            pl.BlockSpec((rows, d), lambda l, j: (0, 0)),
            pl.BlockSpec((1, d, d), lambda l, j: (l, 0, j)),
            pl.BlockSpec((1, 1, d), lambda l, j: (l, 0, j)),
        ],
        out_specs=pl.BlockSpec((1, 1, rows, d), lambda l, j: (l, j, 0, 0)),
        out_shape=jax.ShapeDtypeStruct((depth, n // d, rows, d), F32),
        compiler_params=_params("arbitrary", "arbitrary"),
        name="ada",
    )(c_all, w_ada, b_ada.reshape(depth, 1, n))


def _mod_spec(mod, tiles_per_group):
    _, three, rows, d = mod.shape
    return pl.BlockSpec((1, three, rows, d), lambda i, *_: (i // tiles_per_group, 0, 0, 0))


def _tile(n, pref):
    return pref if n % pref == 0 else n


def _ffn_kernel(x_ref, g_ref, mod_ref, win_ref, wout_ref, o_ref, *, fc):
    x = x_ref[...]
    h = _mod_norm(x, g_ref[...], mod_ref[0, 1], mod_ref[0, 0]).astype(BF16)
    f = wout_ref.shape[0]

    def gate_up(c):
        return (_dot(h, win_ref[:, c * fc:(c + 1) * fc]), _dot(h, win_ref[:, f + c * fc:f + (c + 1) * fc]))

    acc = None
    nxt = gate_up(0)
    for c in range(f // fc):
        a, u = nxt
        if (c + 1) * fc < f:
            nxt = gate_up(c + 1)
        act = (a * jax.nn.sigmoid(a) * u).astype(BF16)
        part = _dot(act, wout_ref[c * fc:(c + 1) * fc, :])
        acc = part if acc is None else acc + part
    o_ref[...] = x + (0.5 * mod_ref[0, 2]) * acc


def _ffn_call(x, g, mod, w_in, w_out, which, tiles_per_group, tm):
    n, d = x.shape
    f = w_out.shape[2]
    fc = FFN_CHUNK if f % FFN_CHUNK == 0 else f
    layer, k = which
    return pl.pallas_call(
        functools.partial(_ffn_kernel, fc=fc),
        grid=(n // tm,),
        in_specs=[
            pl.BlockSpec((tm, d), lambda i: (i, 0)),
            pl.BlockSpec((1, d), lambda i: (0, 0)),
            _mod_spec(mod, tiles_per_group),
            pl.BlockSpec((None, None, d, 2 * f), lambda i: (layer, k, 0, 0)),
            pl.BlockSpec((None, None, f, d), lambda i: (layer, k, 0, 0)),
        ],
        out_specs=pl.BlockSpec((tm, d), lambda i: (i, 0)),
        out_shape=jax.ShapeDtypeStruct((n, d), F32),
        compiler_params=_params("arbitrary"),
        name="ffn",
    )(x, g, mod, w_in, w_out)


def _s5_discretize(a_re, a_im, log_dt, b_re, b_im):
    dt = jnp.exp(log_dt)[:, None]
    mag = jnp.exp(a_re * dt)
    ab_re, ab_im = mag * jnp.cos(a_im * dt), mag * jnp.sin(a_im * dt)
    den = a_re * a_re + a_im * a_im
    n_re, n_im = ab_re - 1.0, ab_im
    w_re = (n_re * a_re + n_im * a_im) / den
    w_im = (n_im * a_re - n_re * a_im) / den
    bb_re = w_re[..., None] * b_re - w_im[..., None] * b_im
    bb_im = w_re[..., None] * b_im + w_im[..., None] * b_re
    return ab_re, ab_im, bb_re, bb_im


def _s5_weights(a_re, a_im, log_dt, b_re, b_im, c_re, c_im, d_skip):
    groups, state = a_re.shape
    ab_re, ab_im, bb_re, bb_im = _s5_discretize(a_re, a_im, log_dt, b_re, b_im)
    nkt = groups // 16
    eye16 = jnp.eye(16, dtype=F32)

    def bd(bb):
        t = bb.transpose(0, 2, 1).reshape(nkt, 16, SSM_GROUP, state)
        full = t[:, :, :, None, :] * eye16[None, :, None, :, None]
        return full.reshape(nkt, 16 * SSM_GROUP, 16 * state).astype(BF16)

    nj = groups // 8
    sel = (jnp.arange(16)[None, None, :] ==
           (jnp.arange(8)[None, :, None] + 8 * (jnp.arange(nj)[:, None, None] % 2))).astype(F32)

    def cd(cc):
        t = cc.transpose(0, 2, 1).reshape(nj, 8, state, SSM_GROUP)
        full = t[:, :, :, None, :] * sel[:, :, None, :, None]
        return full.reshape(nj, 8 * state, 16 * SSM_GROUP).astype(BF16)

    return dict(bd_re=bd(bb_re), bd_im=bd(bb_im), cd_re=cd(c_re), cd_im=cd(c_im),
                a_re=ab_re.reshape(1, groups * state), a_im=ab_im.reshape(1, groups * state),
                dd=d_skip.reshape(1, -1))


def _s5_tail(x, u, y_parts, dd, gate, wglu):
    y = jnp.concatenate(y_parts, axis=1) + dd * u
    z = _dot(jax.nn.gelu(y).astype(BF16), wglu)
    half = z.shape[1] // 2
    return x + gate * (z[:, :half] * jax.nn.sigmoid(z[:, half:]))


def _s5_prompt_kernel(x_ref, g_ref, mod_ref, win_ref, bdre_ref, bdim_ref, are_ref, aim_ref, cdre_ref, cdim_ref,
                      dd_ref, wglu_ref, o_ref, hre_ref, him_ref, sre_scr, sim_scr, cre_scr, cim_scr, *, t_len, pitch):
    @pl.when(pl.program_id(1) == 0)
    def _():
        cre_scr[...] = jnp.zeros_like(cre_scr)
        cim_scr[...] = jnp.zeros_like(cim_scr)

    x = x_ref[...]
    h = _mod_norm(x, g_ref[...], mod_ref[0, 1], mod_ref[0, 0]).astype(BF16)
    u = _dot(h, win_ref[...])
    ub = u.astype(BF16)
    nkt, kw, sw = bdre_ref.shape
    per_kt = sw // LANES
    nvreg = nkt * per_kt // SUBLANES
    for kt in range(nkt):
        ukt = ub[:, kt * kw:(kt + 1) * kw]
        bre = _dot(ukt, bdre_ref[kt])
        bim = _dot(ukt, bdim_ref[kt])
        for m in range(per_kt):
            r0 = (kt * per_kt + m) * pitch
            sre_scr[r0:r0 + t_len, :] = bre[:, m * LANES:(m + 1) * LANES]
            sim_scr[r0:r0 + t_len, :] = bim[:, m * LANES:(m + 1) * LANES]

    vrows = lambda k: slice(k * SUBLANES, (k + 1) * SUBLANES)
    a_re = [are_ref[vrows(k), :] for k in range(nvreg)]
    a_im = [aim_ref[vrows(k), :] for k in range(nvreg)]

    def step(t, carry):
        new = []
        for k in range(nvreg):
            sr, si = carry[2 * k], carry[2 * k + 1]
            rows = pl.ds(k * SUBLANES * pitch + t, SUBLANES, stride=pitch)
            nr = a_re[k] * sr - a_im[k] * si + sre_scr[rows, :]
            ni = a_re[k] * si + a_im[k] * sr + sim_scr[rows, :]
            sre_scr[rows, :] = nr
            sim_scr[rows, :] = ni
            new += [nr, ni]
        return tuple(new)

    init = []
    for k in range(nvreg):
        init += [cre_scr[vrows(k), :], cim_scr[vrows(k), :]]
    fin = lax.fori_loop(0, t_len, step, tuple(init), unroll=8)
    for k in range(nvreg):
        cre_scr[vrows(k), :] = fin[2 * k]
        cim_scr[vrows(k), :] = fin[2 * k + 1]
        hre_ref[0, vrows(k), :] = fin[2 * k]
        him_ref[0, vrows(k), :] = fin[2 * k + 1]

    y_parts = []
    for kt in range(nkt):
        slabs = lambda scr: jnp.concatenate(
            [scr[(kt * per_kt + m) * pitch:(kt * per_kt + m) * pitch + t_len, :] for m in range(per_kt)], axis=1)
        y_parts.append(_dot(slabs(sre_scr).astype(BF16), cdre_ref[kt]) - _dot(slabs(sim_scr).astype(BF16), cdim_ref[kt]))
    o_ref[...] = _s5_tail(x, u, y_parts, dd_ref[...], mod_ref[0, 2], wglu_ref[...])


def _s5_prompt_call(x, g, mod, w_in, sw, w_glu, bsz, t_len):
    n, d = x.shape
    seq = n // bsz
    nt = seq // t_len
    pitch = t_len + SUBLANES
    nstate = sw["a_re"].shape[1]
    nslab = nstate // LANES
    nkt = sw["bd_re"].shape[0]
    a_re = sw["a_re"].reshape(nslab, LANES)
    a_im = sw["a_im"].reshape(nslab, LANES)
    cd_re = sw["cd_re"].reshape(nkt, nstate // nkt, -1)
    cd_im = sw["cd_im"].reshape(nkt, nstate // nkt, -1)
    const = lambda *shape: pl.BlockSpec(shape, lambda b, t: (0,) * len(shape))
    out, hre, him = pl.pallas_call(
        functools.partial(_s5_prompt_kernel, t_len=t_len, pitch=pitch),
        grid=(bsz, nt),
        in_specs=[
            pl.BlockSpec((t_len, d), lambda b, t: (b * nt + t, 0)),
            const(1, d),
            pl.BlockSpec((1, 3, 1, d), lambda b, t: (b, 0, 0, 0)),
            const(*w_in.shape),
            const(*sw["bd_re"].shape), const(*sw["bd_im"].shape),
            const(nslab, LANES), const(nslab, LANES),
            const(*cd_re.shape), const(*cd_im.shape),
            const(1, d),
            const(*w_glu.shape),
        ],
        out_specs=[
            pl.BlockSpec((t_len, d), lambda b, t: (b * nt + t, 0)),
            pl.BlockSpec((1, nslab, LANES), lambda b, t: (b, 0, 0)),
            pl.BlockSpec((1, nslab, LANES), lambda b, t: (b, 0, 0)),
        ],
        out_shape=[jax.ShapeDtypeStruct((n, d), F32),
                   jax.ShapeDtypeStruct((bsz, nslab, LANES), F32),
                   jax.ShapeDtypeStruct((bsz, nslab, LANES), F32)],
        scratch_shapes=[pltpu.VMEM((nslab * pitch, LANES), F32), pltpu.VMEM((nslab * pitch, LANES), F32),
                        pltpu.VMEM((nslab, LANES), F32), pltpu.VMEM((nslab, LANES), F32)],
        compiler_params=_params("arbitrary", "arbitrary"),
        name="s5_prompt",
    )(x, g, mod, w_in, sw["bd_re"], sw["bd_im"], a_re, a_im, cd_re, cd_im, sw["dd"], w_glu)
    return out, hre, him


def _s5_sample_kernel(x_ref, g_ref, mod_ref, win_ref, bdre_ref, bdim_ref, are_ref, aim_ref, h0re_ref, h0im_ref,
                      cdre_ref, cdim_ref, dd_ref, wglu_ref, o_ref, hre_ref, him_ref, sre_scr, sim_scr):
    nt, bs, d = x_ref.shape
    x = x_ref[...].reshape(nt * bs, d)
    mods = [jnp.concatenate([mod_ref[0, k]] * nt, axis=0) for k in range(3)]
    h = _mod_norm(x, g_ref[...], mods[1], mods[0]).astype(BF16)
    u = _dot(h, win_ref[...])
    ub = u.astype(BF16)
    nkt = bdre_ref.shape[0]
    kw = bdre_ref.shape[1]
    sw = bdre_ref.shape[2]
    for kt in range(nkt):
        ukt = ub[:, kt * kw:(kt + 1) * kw]
        sre_scr[:, kt * sw:(kt + 1) * sw] = _dot(ukt, bdre_ref[kt])
        sim_scr[:, kt * sw:(kt + 1) * sw] = _dot(ukt, bdim_ref[kt])
    a_re = are_ref[...]
    a_im = aim_ref[...]
    sr = h0re_ref[...]
    si = h0im_ref[...]
    for t in range(nt):
        rows = slice(t * bs, (t + 1) * bs)
        nr = a_re * sr - a_im * si + sre_scr[rows, :]
        ni = a_re * si + a_im * sr + sim_scr[rows, :]
        sre_scr[rows, :] = nr
        sim_scr[rows, :] = ni
        sr, si = nr, ni
    hre_ref[...] = sr
    him_ref[...] = si
    y_parts = []
    for kt in range(nkt):
        cols = slice(kt * sw, (kt + 1) * sw)
        y_parts.append(_dot(sre_scr[:, cols].astype(BF16), cdre_ref[kt])
                       - _dot(sim_scr[:, cols].astype(BF16), cdim_ref[kt]))
    out = _s5_tail(x, u, y_parts, dd_ref[...], mods[2], wglu_ref[...])
    o_ref[...] = out.reshape(nt, bs, d)


def _s5_sample_call(x_tm, g, mod, w_in, sw, w_glu, h0_re, h0_im, bs):
    nt, nb, d = x_tm.shape
    nstate = sw["a_re"].shape[1]
    nkt = sw["bd_re"].shape[0]
    cd_re = sw["cd_re"].reshape(nkt, nstate // nkt, -1)
    cd_im = sw["cd_im"].reshape(nkt, nstate // nkt, -1)
    const = lambda *shape: pl.BlockSpec(shape, lambda i: (0,) * len(shape))
    return pl.pallas_call(
        _s5_sample_kernel,
        grid=(nb // bs,),
        in_specs=[
            pl.BlockSpec((nt, bs, d), lambda i: (0, i, 0)),
            const(1, d),
            pl.BlockSpec((1, 3, bs, d), lambda i: (i, 0, 0, 0)),
            const(*w_in.shape),
            const(*sw["bd_re"].shape), const(*sw["bd_im"].shape),
            const(1, nstate), const(1, nstate),
            pl.BlockSpec((bs, nstate), lambda i: (i, 0)),
            pl.BlockSpec((bs, nstate), lambda i: (i, 0)),
            const(*cd_re.shape), const(*cd_im.shape),
            const(1, d),
            const(*w_glu.shape),
        ],
        out_specs=[
            pl.BlockSpec((nt, bs, d), lambda i: (0, i, 0)),
            pl.BlockSpec((bs, nstate), lambda i: (i, 0)),
            pl.BlockSpec((bs, nstate), lambda i: (i, 0)),
        ],
        out_shape=[jax.ShapeDtypeStruct((nt, nb, d), F32),
                   jax.ShapeDtypeStruct((nb, nstate), F32),
                   jax.ShapeDtypeStruct((nb, nstate), F32)],
        scratch_shapes=[pltpu.VMEM((nt * bs, nstate), F32), pltpu.VMEM((nt * bs, nstate), F32)],
        compiler_params=_params("arbitrary"),
        name="s5_sample",
    )(x_tm, g, mod, w_in, sw["bd_re"], sw["bd_im"], sw["a_re"], sw["a_im"], h0_re, h0_im, cd_re, cd_im,
      sw["dd"], w_glu)


def _mla_weights(w_a, w_uq, w_ukv):
    d = w_a.shape[0]
    w_aq = w_a[:, :Q_LORA]
    w_akv = w_a[:, Q_LORA:Q_LORA + KV_LORA]
    w_akr = w_a[:, Q_LORA + KV_LORA:]
    kr1, kr2 = w_akr[:, :16], w_akr[:, 16:]
    z64, z32 = jnp.zeros((d, QK_NOPE), F32), jnp.zeros((d, 32), F32)
    wkr = jnp.concatenate([z64, kr1, kr2, z32], axis=1)
    wkr_sw = jnp.concatenate([z64, -kr2, kr1, z32], axis=1)
    wq = w_uq.reshape(Q_LORA, HEADS, QK_HEAD)
    qn, q1, q2 = wq[..., :QK_NOPE], wq[..., QK_NOPE:QK_NOPE + 16], wq[..., QK_NOPE + 16:]
    wuq = jnp.concatenate([qn, q1, q2, q2, q1], axis=-1).reshape(Q_LORA, HEADS * HEAD_PAD)
    wuq_sw = jnp.concatenate([jnp.zeros_like(qn), -q2, q1, q1, -q2], axis=-1).reshape(Q_LORA, HEADS * HEAD_PAD)
    wkv = w_ukv.reshape(KV_LORA, HEADS, QK_NOPE + V_HEAD)
    wk, wv = wkv[..., :QK_NOPE], wkv[..., QK_NOPE:]
    wuk = jnp.concatenate([wk, jnp.zeros_like(wk)], axis=-1).reshape(KV_LORA, HEADS * HEAD_PAD)
    wuv = wv.reshape(KV_LORA, HEADS * V_HEAD)
    wuk_t = wk.transpose(1, 2, 0)
    wabs = jnp.concatenate([wuk_t, jnp.zeros_like(wuk_t)], axis=1)
    eye_h = jnp.eye(HEADS, dtype=F32)
    wuv_bd = (wv.transpose(1, 0, 2)[:, :, None, :] * eye_h[:, None, :, None]).reshape(HEADS * KV_LORA, HEADS * V_HEAD)
    c = lambda w: w.astype(BF16)
    return dict(w_aq=c(w_aq), w_akv=c(w_akv), w_akr=c(w_akr), wkr=c(wkr), wkr_sw=c(wkr_sw), wuq=c(wuq),
                wuq_sw=c(wuq_sw), wuk=c(wuk), wuv=c(wuv),
                wuk_t=c(wk.transpose(2, 1, 0).reshape(QK_NOPE * HEADS, KV_LORA)),
                wabs=c(wabs), wuv_bd=c(wuv_bd))


def _rope_cos_sin(pos):
    inv = 1.0 / (ROPE_THETA ** (jnp.arange(0, QK_ROPE, 2, dtype=F32) / QK_ROPE))
    ang = pos.astype(F32)[:, None] * inv[None, :]
    return jnp.cos(ang), jnp.sin(ang)


def _rope_tables(pos, g, scale, with_partner):
    c, s = _rope_cos_sin(pos)
    n = pos.shape[0]
    gn, g1, g2 = g[:QK_NOPE], g[QK_NOPE:QK_NOPE + 16], g[QK_NOPE + 16:]
    ones = jnp.ones((n, 1), F32)
    z16, z64 = jnp.zeros((n, 16), F32), jnp.zeros((n, QK_NOPE), F32)
    if with_partner:
        gc = jnp.concatenate([ones * gn, c * g1, c * g2, c * g2, -(c * g1)], axis=1)
        gs = jnp.concatenate([z64, s * g2, s * g1, s * g1, -(s * g2)], axis=1)
    else:
        gc = jnp.concatenate([ones * gn, c * g1, c * g2, z16, z16], axis=1)
        gs = jnp.concatenate([z64, s * g2, s * g1, z16, z16], axis=1)
    return gc * scale, gs * scale


def _mla_proj_kernel(x_ref, g_ref, mod_ref, waq_ref, wakv_ref, wakr_ref, wkr_ref, wkrsw_ref, gqa_ref, gkva_ref,
                     wuq_ref, wuqsw_ref, wuk_ref, wuv_ref, gcq_ref, gsq_ref, gck_ref, gsk_ref,
                     q_ref, k_ref, v_ref, ckv_ref, kr_ref, *, with_kv):
    x = x_ref[...]
    h = _mod_norm(x, g_ref[...], mod_ref[0, 1], mod_ref[0, 0]).astype(BF16)
    aq = _dot(h, waq_ref[...])
    akv = _dot(h, wakv_ref[...])
    kr_ref[...] = _dot(h, wakr_ref[...])
    cq = (aq * lax.rsqrt(jnp.mean(aq * aq, axis=-1, keepdims=True) + EPS) * gqa_ref[...]).astype(BF16)
    ckv = akv * lax.rsqrt(jnp.mean(akv * akv, axis=-1, keepdims=True) + EPS) * gkva_ref[...]
    ckv_ref[...] = ckv
    q_pre = _dot(cq, wuq_ref[...])
    q_sw = _dot(cq, wuqsw_ref[...])
    tm = x.shape[0]
    in_head = lax.broadcasted_iota(jnp.int32, (tm, HEAD_PAD), 1) < QK_HEAD
    gcq, gsq = gcq_ref[...], gsq_ref[...]
    for hd in range(HEADS):
        cols = slice(hd * HEAD_PAD, (hd + 1) * HEAD_PAD)
        qp = q_pre[:, cols]
        ssq = jnp.sum(jnp.where(in_head, qp * qp, 0.0), axis=-1, keepdims=True)
        rinv = lax.rsqrt(ssq * (1.0 / QK_HEAD) + EPS)
        q_ref[:, cols] = (rinv * (qp * gcq + q_sw[:, cols] * gsq)).astype(q_ref.dtype)
    if with_kv:
        ckv_b = ckv.astype(BF16)
        k_pre = _dot(ckv_b, wuk_ref[...])
        v_all = _dot(ckv_b, wuv_ref[...])
        pair_w = 2 * V_HEAD
        one_col = (lax.broadcasted_iota(jnp.int32, (tm, pair_w), 1) == 0).astype(v_ref.dtype)
        for hp in range(HEADS // 2):
            v_ref[:, hp * 2 * pair_w:hp * 2 * pair_w + pair_w] = v_all[:, hp * pair_w:(hp + 1) * pair_w].astype(v_ref.dtype)
            v_ref[:, hp * 2 * pair_w + pair_w:(hp + 1) * 2 * pair_w] = one_col
        kr_slab = _dot(h, wkr_ref[...])
        kr_sw = _dot(h, wkrsw_ref[...])
        gck = gck_ref[...]
        kr_rot = kr_sw * gsk_ref[...]
        for hd in range(HEADS):
            cols = slice(hd * HEAD_PAD, (hd + 1) * HEAD_PAD)
            kp = k_pre[:, cols] + kr_slab
            ssq = jnp.sum(kp * kp, axis=-1, keepdims=True)
            rinv = lax.rsqrt(ssq * (1.0 / QK_HEAD) + EPS)
            k_ref[:, cols] = (rinv * (kp * gck + kr_rot)).astype(k_ref.dtype)
    else:
        k_ref[...] = jnp.zeros_like(k_ref)
        v_ref[...] = jnp.zeros_like(v_ref)


def _mla_proj_call(x, g, mod, mw, g_qa, g_kva, tabs, tiles_per_group, tm, with_kv, q_dtype):
    n, d = x.shape
    gcq, gsq, gck, gsk = tabs
    ntab = gcq.shape[0] // tm
    const = lambda a: pl.BlockSpec(a.shape, lambda i: (0,) * a.ndim)
    tab = pl.BlockSpec((tm, HEAD_PAD), lambda i: (i % ntab, 0))
    row = lambda w: pl.BlockSpec((tm, w), lambda i: (i, 0))
    kv_rows = tm if with_kv else SUBLANES
    kv_n = n if with_kv else SUBLANES * (n // tm)
    kvrow = lambda w: pl.BlockSpec((kv_rows, w), lambda i: (i, 0))
    g_qa2, g_kva2 = g_qa.reshape(1, -1), g_kva.reshape(1, -1)
    ws = [mw["w_aq"], mw["w_akv"], mw["w_akr"], mw["wkr"], mw["wkr_sw"], g_qa2, g_kva2,
          mw["wuq"], mw["wuq_sw"], mw["wuk"], mw["wuv"]]
    return pl.pallas_call(
        functools.partial(_mla_proj_kernel, with_kv=with_kv),
        grid=(n // tm,),
        in_specs=[row(d), pl.BlockSpec((1, d), lambda i: (0, 0)), _mod_spec(mod, tiles_per_group)]
                 + [const(w) for w in ws] + [tab, tab, tab, tab],
        out_specs=[row(HEADS * HEAD_PAD), kvrow(HEADS * HEAD_PAD), kvrow(2 * HEADS * V_HEAD), row(KV_LORA),
                   row(QK_ROPE)],
        out_shape=[jax.ShapeDtypeStruct((n, HEADS * HEAD_PAD), q_dtype),
                   jax.ShapeDtypeStruct((kv_n, HEADS * HEAD_PAD), BF16),
                   jax.ShapeDtypeStruct((kv_n, 2 * HEADS * V_HEAD), BF16),
                   jax.ShapeDtypeStruct((n, KV_LORA), F32),
                   jax.ShapeDtypeStruct((n, QK_ROPE), F32)],
        compiler_params=_params("arbitrary"),
        name="mla_proj",
    )(x, g, mod, *ws, gcq, gsq, gck, gsk)


def _flash_kernel(qi_ref, ki_ref, q_ref, k_ref, v_ref, o_ref, m_scr, acc_scr):
    p_id = pl.program_id(2)
    qi, ki = qi_ref[p_id], ki_ref[p_id]
    tq, tk = q_ref.shape[0], k_ref.shape[0]

    @pl.when(ki == 0)
    def _():
        m_scr[...] = jnp.full_like(m_scr, NEG_INF)
        acc_scr[...] = jnp.zeros_like(acc_scr)

    rb = min(tq, FLASH_ROWS)
    n_heads = m_scr.shape[0]
    pair_w = 2 * V_HEAD

    def update(masked):
        units = [(i, hh) for i in range(tq // rb) for hh in range(n_heads)]

        def n_keys(i):
            return (i + 1) * rb if masked else tk

        def scores(i, hh):
            cols = slice(hh * HEAD_PAD, (hh + 1) * HEAD_PAD)
            return _dot_nt(q_ref[i * rb:(i + 1) * rb, cols], k_ref[0:n_keys(i), cols])

        def finish(i, hh, s):
            rows = slice(i * rb, (i + 1) * rb)
            nk = n_keys(i)
            if masked:
                visible = (lax.broadcasted_iota(jnp.int32, (rb, nk), 1)
                           <= i * rb + lax.broadcasted_iota(jnp.int32, (rb, nk), 0))
                s = jnp.where(visible, s, NEG_INF)
            m_prev = m_scr[hh, rows, :]
            m_new = jnp.maximum(m_prev, jnp.max(s, axis=1, keepdims=True))
            alpha = jnp.exp2(m_prev - m_new)
            p = jnp.exp2((s - m_new).astype(BF16))
            v_cols = slice((hh // 2) * 2 * pair_w, (hh // 2 + 1) * 2 * pair_w)
            acc_scr[hh, rows, :] = alpha * acc_scr[hh, rows, :] + _dot(p, v_ref[0:nk, v_cols])
            m_scr[hh, rows, :] = m_new

        s_next = scores(*units[0])
        for idx, unit in enumerate(units):
            s_cur = s_next
            if idx + 1 < len(units):
                s_next = scores(*units[idx + 1])
            finish(*unit, s_cur)

    @pl.when(ki < qi)
    def _():
        update(False)

    @pl.when(ki == qi)
    def _():
        update(True)

    @pl.when(ki == qi)
    def _():
        lane = lax.broadcasted_iota(jnp.int32, (tq, pair_w), 1)
        for pair in range(n_heads // 2):
            out = [acc_scr[hh, :, 0:pair_w] / acc_scr[hh, :, pair_w:pair_w + 1] for hh in (2 * pair, 2 * pair + 1)]
            o_ref[:, pair * pair_w:(pair + 1) * pair_w] = jnp.where(lane < V_HEAD, out[0], out[1]).astype(o_ref.dtype)


def _flash_call(q, k, v, bsz, tq):
    n = q.shape[0]
    seq = n // bsz
    nq = seq // tq
    nh = FLASH_HEADS
    pairs = [(a, b) for a in range(nq) for b in range(a + 1)]
    qi_tab = jnp.asarray([p[0] for p in pairs], jnp.int32)
    ki_tab = jnp.asarray([p[1] for p in pairs], jnp.int32)
    grid_spec = pltpu.PrefetchScalarGridSpec(
        num_scalar_prefetch=2,
        grid=(bsz, HEADS // nh, len(pairs)),
        in_specs=[
            pl.BlockSpec((tq, nh * HEAD_PAD), lambda b, hp, p, qi, ki: (b * nq + qi[p], hp)),
            pl.BlockSpec((tq, nh * HEAD_PAD), lambda b, hp, p, qi, ki: (b * nq + ki[p], hp)),
            pl.BlockSpec((tq, nh * 2 * V_HEAD), lambda b, hp, p, qi, ki: (b * nq + ki[p], hp)),
        ],
        out_specs=pl.BlockSpec((tq, nh * V_HEAD), lambda b, hp, p, qi, ki: (b * nq + qi[p], hp)),
        scratch_shapes=[pltpu.VMEM((nh, tq, 1), F32), pltpu.VMEM((nh, tq, 4 * V_HEAD), F32)],
    )
    return pl.pallas_call(
        _flash_kernel,
        grid_spec=grid_spec,
        out_shape=jax.ShapeDtypeStruct((n, HEADS * V_HEAD), BF16),
        compiler_params=_params("arbitrary", "arbitrary", "arbitrary"),
        name="flash_prompt",
    )(qi_tab, ki_tab, q, k, v)


def _absorb_kernel(q_ref, gk_ref, wabs_ref, o_ref):
    gk = gk_ref[...]
    for hd in range(HEADS):
        qn = (q_ref[:, hd * HEAD_PAD:(hd + 1) * HEAD_PAD] * gk).astype(BF16)
        o_ref[:, hd * KV_LORA:(hd + 1) * KV_LORA] = _dot(qn, wabs_ref[hd]).astype(o_ref.dtype)


def _absorb_call(q, gk_slab, wabs):
    n = q.shape[0]
    return pl.pallas_call(
        _absorb_kernel,
        grid=(1,),
        in_specs=[pl.BlockSpec(q.shape, lambda i: (0, 0)), pl.BlockSpec(gk_slab.shape, lambda i: (0, 0)),
                  pl.BlockSpec(wabs.shape, lambda i: (0, 0, 0))],
        out_specs=pl.BlockSpec((n, HEADS * KV_LORA), lambda i: (0, 0)),
        out_shape=jax.ShapeDtypeStruct((n, HEADS * KV_LORA), BF16),
        compiler_params=_params("arbitrary"),
        name="absorb_q",
    )(q, gk_slab, wabs)


def _chunk_scores(lhs, qx, latb, krt, gt):
    kq = _dot_nt(lhs, latb)
    half = krt.shape[0]
    xt = jnp.concatenate([krt * gt[:half], krt * gt[half:]], axis=0).astype(BF16)
    kx = _dot(qx, xt)
    return kq, kx, jnp.sum(krt * krt, axis=0, keepdims=True)


def _chunk_update(kq, kx, ssq_rope, latb, mask, m_scr, l_scr, acc_scr):
    nk = HEADS * QK_NOPE
    n = latb.shape[0]
    kn = kq[:nk]
    ssq = jnp.sum((kn * kn).reshape(QK_NOPE, HEADS, n), axis=0) + ssq_rope
    rinv = lax.rsqrt(ssq * (1.0 / QK_HEAD) + EPS)
    nrow = kx.shape[0]
    s = (kq[nk:] + kx) * jnp.concatenate([rinv] * (nrow // HEADS), axis=0)
    if mask is not None:
        s = jnp.where(mask, s, NEG_INF)
    m_prev = m_scr[...]
    m_new = jnp.maximum(m_prev, jnp.max(s, axis=1, keepdims=True))
    alpha = jnp.exp2(m_prev - m_new)
    p = jnp.exp2(s - m_new)
    l_scr[...] = alpha * l_scr[...] + jnp.sum(p, axis=1, keepdims=True)
    acc_scr[...] = alpha * acc_scr[...] + _dot(p.astype(BF16), latb)
    m_scr[...] = m_new


def _sample_attn_kernel(pt_ref, qa_ref, qx_ref, wukt_ref, lat_hbm, krt_hbm, gt_ref, latn_ref, krn_ref, gtn_ref, o_ref,
                        lat_buf, krt_buf, lat_sem, krt_sem, lhs_scr, m_scr, l_scr, acc_scr,
                        *, layer, pages, chunk, n_new):
    j = pl.program_id(1)
    nj = pl.num_programs(1)
    step = pl.program_id(0) * nj + j
    slot = step % 2
    nk = wukt_ref.shape[0]
    nrow = qa_ref.shape[0]

    def page_copies(step_, slot_, which=range(pages)):
        copies = []
        for k in which:
            page = pt_ref[step_ * pages + k]
            copies.append(pltpu.make_async_copy(lat_hbm.at[layer, page],
                                                lat_buf.at[slot_, pl.ds(k * PAGE, PAGE), :], lat_sem.at[slot_]))
            copies.append(pltpu.make_async_copy(krt_hbm.at[layer, page],
                                                krt_buf.at[slot_, :, pl.ds(k * PAGE, PAGE)], krt_sem.at[slot_]))
        return copies

    @pl.when(step == 0)
    def _():
        for cp in page_copies(0, 0):
            cp.start()

    last_step = pl.num_programs(0) * nj - 1
    next_step = jnp.minimum(step + 1, last_step)

    @pl.when(j == 0)
    def _():
        lhs_scr[0:nk, :] = wukt_ref[...]
        lhs_scr[nk:nk + nrow, :] = qa_ref[...]
        m_scr[...] = jnp.full_like(m_scr, NEG_INF)
        l_scr[...] = jnp.zeros_like(l_scr)
        acc_scr[...] = jnp.zeros_like(acc_scr)

    for cp in page_copies(step, slot):
        cp.wait()
    lat_cur = lat_buf.at[slot]
    kr_cur = krt_buf.at[slot]
    lhs = lhs_scr[...]
    qx = qx_ref[...]

    def stage_a(c):
        keys = slice(c * chunk, (c + 1) * chunk)
        latb = lat_cur[keys, :].astype(BF16)
        return _chunk_scores(lhs, qx, latb, kr_cur[:, keys], gt_ref[:, keys]) + (latb,)

    def stage_new():
        latb = latn_ref[...].astype(BF16)
        return _chunk_scores(lhs, qx, latb, krn_ref[...], gtn_ref[...]) + (latb,)

    is_last = j == nj - 1
    t_row = lax.broadcasted_iota(jnp.int32, (nrow, PAGE), 0) // HEADS
    kk = lax.broadcasted_iota(jnp.int32, (nrow, PAGE), 1)
    new_mask = (kk < n_new) & (kk <= t_row) & is_last

    n_chunks = pages * PAGE // chunk
    pages_per_chunk = pages // n_chunks
    stages = [functools.partial(stage_a, c) for c in range(n_chunks)] + [stage_new]
    masks = [None] * n_chunks + [new_mask]
    depth = min(SAMPLE_DEPTH, len(stages))
    queue = [stages[c]() for c in range(depth)]
    for c in range(len(stages)):
        cur = queue.pop(0)
        if c + depth < len(stages):
            queue.append(stages[c + depth]())
        if c < n_chunks:
            for cp in page_copies(next_step, 1 - slot, range(c * pages_per_chunk, (c + 1) * pages_per_chunk)):
                cp.start()
        _chunk_update(*cur, masks[c], m_scr, l_scr, acc_scr)

    @pl.when(step == last_step)
    def _():
        for cp in page_copies(next_step, 1 - slot):
            cp.wait()

    @pl.when(is_last)
    def _():
        o_ref[...] = acc_scr[...] / l_scr[...]


def _sample_attn_call(page_table, qabs, qx, wuk_t, cache_latent, cache_krope_t, layer, gt, lat_new, krt_new, gt_new,
                      n_new, pages):
    nb, n_pages = page_table.shape
    nj = n_pages // pages
    nrow = qabs.shape[1]
    chunk = min(pages * PAGE, SAMPLE_CHUNK)
    pt_flat = page_table.reshape(-1)
    keys = pages * PAGE
    grid_spec = pltpu.PrefetchScalarGridSpec(
        num_scalar_prefetch=1,
        grid=(nb, nj),
        in_specs=[
            pl.BlockSpec((None, nrow, KV_LORA), lambda b, j, pt: (b, 0, 0)),
            pl.BlockSpec((None,) + qx.shape[1:], lambda b, j, pt: (b, 0, 0)),
            pl.BlockSpec(wuk_t.shape, lambda b, j, pt: (0, 0)),
            pl.BlockSpec(memory_space=pl.ANY),
            pl.BlockSpec(memory_space=pl.ANY),
            pl.BlockSpec((2 * QK_ROPE, keys), lambda b, j, pt: (0, j)),
            pl.BlockSpec((None, PAGE, KV_LORA), lambda b, j, pt: (b, 0, 0)),
            pl.BlockSpec((None, QK_ROPE, PAGE), lambda b, j, pt: (b, 0, 0)),
            pl.BlockSpec((2 * QK_ROPE, PAGE), lambda b, j, pt: (0, 0)),
        ],
        out_specs=pl.BlockSpec((None, nrow, KV_LORA), lambda b, j, pt: (b, 0, 0)),
        scratch_shapes=[pltpu.VMEM((2, keys, KV_LORA), F32),
                        pltpu.VMEM((2, QK_ROPE, keys), F32),
                        pltpu.SemaphoreType.DMA((2,)),
                        pltpu.SemaphoreType.DMA((2,)),
                        pltpu.VMEM((wuk_t.shape[0] + nrow, KV_LORA), BF16),
                        pltpu.VMEM((nrow, 1), F32), pltpu.VMEM((nrow, 1), F32),
                        pltpu.VMEM((nrow, KV_LORA), F32)],
    )
    return pl.pallas_call(
        functools.partial(_sample_attn_kernel, layer=layer, pages=pages, chunk=chunk, n_new=n_new),
        grid_spec=grid_spec,
        out_shape=jax.ShapeDtypeStruct((nb, nrow, KV_LORA), F32),
        compiler_params=_params("arbitrary", "arbitrary"),
        name="sample_attn",
    )(pt_flat, qabs, qx, wuk_t, cache_latent, cache_krope_t, gt, lat_new, krt_new, gt_new)


def _attn_out_kernel(x_ref, a_ref, mod_ref, *rest, pre_proj):
    if pre_proj:
        wuv_ref, wo_ref, o_ref = rest
        a = _dot(a_ref[...].astype(BF16), wuv_ref[...]).astype(BF16)
    else:
        wo_ref, o_ref = rest
        a = a_ref[...].astype(BF16)
    o_ref[...] = x_ref[...] + mod_ref[0, 2] * _dot(a, wo_ref[...])


def _attn_out_call(x, a, mod, w_o, tiles_per_group, tm, wuv_bd=None):
    n, d = x.shape
    ws = [w_o] if wuv_bd is None else [wuv_bd, w_o]
    return pl.pallas_call(
        functools.partial(_attn_out_kernel, pre_proj=wuv_bd is not None),
        grid=(n // tm,),
        in_specs=[pl.BlockSpec((tm, d), lambda i: (i, 0)), pl.BlockSpec((tm, a.shape[1]), lambda i: (i, 0)),
                  _mod_spec(mod, tiles_per_group)] + [pl.BlockSpec(w.shape, lambda i: (0, 0)) for w in ws],
        out_specs=pl.BlockSpec((tm, d), lambda i: (i, 0)),
        out_shape=jax.ShapeDtypeStruct((n, d), F32),
        compiler_params=_params("arbitrary"),
        name="attn_out",
    )(x, a, mod, *ws)


def kernel(x_prompt, x_sample, c_prompt, c_sample, cache_latent, cache_krope, state_ssm_re, state_ssm_im, page_table, w_ada, b_ada, norm_g, ffn_w_in, ffn_w_out, ssm_w_in, ssm_a_re, ssm_a_im, ssm_log_dt, ssm_b_re, ssm_b_im, ssm_c_re, ssm_c_im, ssm_d, ssm_w_glu, mla_w_a, mla_g_qa, mla_g_kva, mla_w_uq, mla_w_ukv, mla_g_q, mla_g_k, mla_w_o):
    bsz, seq, d = x_prompt.shape
    nb, nt, _ = x_sample.shape
    depth = w_ada.shape[0]
    n_past = page_table.shape[1] * PAGE

    tm_p = _tile(seq, TOKEN_TILE)
    tpg_p = seq // tm_p
    n_s = nb * nt
    s5_t = _tile(seq, S5_CHUNK)
    s5_bs = _tile(nb, S5_SEQS)
    tq = _tile(seq, FLASH_TILE)
    pages = _tile(page_table.shape[1], PAGES_PER_STEP)

    c_all = jnp.concatenate([c_prompt, c_sample], axis=0)
    pad = (-c_all.shape[0]) % SUBLANES
    ada = _ada_call(jnp.pad(c_all, ((0, pad), (0, 0))), w_ada, b_ada)

    xp = x_prompt.reshape(bsz * seq, d)
    xs = x_sample.transpose(1, 0, 2).reshape(n_s, d)

    pos_p = jnp.arange(seq, dtype=jnp.int32)
    pos_s = n_past + jnp.arange(nt, dtype=jnp.int32)
    outs = {k: [] for k in ("lat_p", "kr_p", "lat_s", "kr_s", "hre_p", "him_p", "hre_s", "him_s")}
    w_in = ffn_w_in.astype(BF16)
    w_out = ffn_w_out.astype(BF16)

    for i in range(depth):
        ada_p = ada[i, :, :bsz]
        ada_s = ada[i, :, bsz:bsz + nb]
        mod_p = [ada_p[3 * j:3 * j + 3].transpose(1, 0, 2)[:, :, None, :] for j in range(3)]
        mod_s_seq = [ada_s[3 * j:3 * j + 3] for j in range(3)]
        mod_s = [jnp.tile(m, (1, nt, 1))[None] for m in mod_s_seq]
        g = norm_g[i].reshape(3, 1, d)

        xp = _ffn_call(xp, g[0], mod_p[0], w_in, w_out, (i, 0), tpg_p, tm_p)
        xs = _ffn_call(xs, g[0], mod_s[0], w_in, w_out, (i, 0), 1, n_s)

        j = i // 2
        if i % 2 == 0:
            sw = _s5_weights(ssm_a_re[j], ssm_a_im[j], ssm_log_dt[j], ssm_b_re[j], ssm_b_im[j],
                             ssm_c_re[j], ssm_c_im[j], ssm_d[j])
            w_sin = ssm_w_in[j].astype(BF16)
            w_glu = ssm_w_glu[j].astype(BF16)
            xp, hre, him = _s5_prompt_call(xp, g[1], mod_p[1], w_sin, sw, w_glu, bsz, s5_t)
            groups, state = ssm_a_re[j].shape
            outs["hre_p"].append(hre.reshape(bsz, groups, state))
            outs["him_p"].append(him.reshape(bsz, groups, state))
            mod_blk = mod_s_seq[1].reshape(3, nb // s5_bs, s5_bs, d).transpose(1, 0, 2, 3)
            xs3, hre, him = _s5_sample_call(xs.reshape(nt, nb, d), g[1], mod_blk, w_sin, sw, w_glu,
                                            state_ssm_re[j].reshape(nb, -1), state_ssm_im[j].reshape(nb, -1), s5_bs)
            xs = xs3.reshape(n_s, d)
            outs["hre_s"].append(hre.reshape(nb, groups, state))
            outs["him_s"].append(him.reshape(nb, groups, state))
        else:
            mw = _mla_weights(mla_w_a[j], mla_w_uq[j], mla_w_ukv[j])
            w_o = mla_w_o[j].astype(BF16)
            g_q, g_k = mla_g_q[j], mla_g_k[j]
            tabs_p = _rope_tables(pos_p, g_q, ATTN_SCALE * LOG2E, False) + _rope_tables(pos_p, g_k, 1.0, False)
            q, k, v, ckv, kr = _mla_proj_call(xp, g[1], mod_p[1], mw, mla_g_qa[j], mla_g_kva[j], tabs_p, tpg_p, tm_p,
                                              True, BF16)
            outs["lat_p"].append(ckv.reshape(bsz, seq, KV_LORA))
            outs["kr_p"].append(kr.reshape(bsz, seq, QK_ROPE))
            attn = _flash_call(q, k, v, bsz, tq)
            xp = _attn_out_call(xp, attn, mod_p[1], w_o, tpg_p, tm_p)
            pos_rows = jnp.repeat(pos_s, nb)
            tabs_s = _rope_tables(pos_rows, g_q, ATTN_SCALE * LOG2E, True) + _rope_tables(pos_rows, g_k, 1.0, False)
            q_s, _, _, ckv_s, kr_s = _mla_proj_call(xs, g[1], mod_s[1], mw, mla_g_qa[j], mla_g_kva[j], tabs_s, 1, n_s,
                                                    False, F32)
            lat_new = ckv_s.reshape(nt, nb, KV_LORA).transpose(1, 0, 2)
            kr_new = kr_s.reshape(nt, nb, QK_ROPE).transpose(1, 0, 2)
            outs["lat_s"].append(lat_new)
            outs["kr_s"].append(kr_new)
            q_sm = q_s.reshape(nt, nb, HEADS * HEAD_PAD).transpose(1, 0, 2).reshape(n_s, HEADS * HEAD_PAD)
            gk_slab = jnp.concatenate([g_k[:QK_NOPE], jnp.zeros((HEAD_PAD - QK_NOPE,), F32)]).reshape(1, HEAD_PAD)
            qabs = _absorb_call(q_sm, gk_slab, mw["wabs"]).reshape(nb, nt * HEADS, KV_LORA)
            qx = q_sm.reshape(nb, nt * HEADS, HEAD_PAD)[..., QK_NOPE:].astype(BF16)
            pos_k = jnp.arange(n_past + PAGE, dtype=jnp.int32)
            ck, sk = _rope_cos_sin(pos_k)
            g1, g2 = g_k[QK_NOPE:QK_NOPE + 16], g_k[QK_NOPE + 16:]
            gt = jnp.concatenate([ck * g1, ck * g2, sk * g1, sk * g2], axis=1).T
            padn = ((0, 0), (0, PAGE - nt), (0, 0))
            krt_new = jnp.pad(kr_new, padn).transpose(0, 2, 1)
            ctx = _sample_attn_call(page_table, qabs, qx, mw["wuk_t"], cache_latent, cache_krope.transpose(0, 1, 3, 2),
                                    j, gt[:, :n_past], jnp.pad(lat_new, padn), krt_new, gt[:, n_past:], nt, pages)
            ctx_tm = ctx.reshape(nb, nt, HEADS * KV_LORA).transpose(1, 0, 2).reshape(n_s, HEADS * KV_LORA)
            xs = _attn_out_call(xs, ctx_tm, mod_s[1], w_o, 1, n_s, wuv_bd=mw["wuv_bd"])

        xp = _ffn_call(xp, g[2], mod_p[2], w_in, w_out, (i, 1), tpg_p, tm_p)
        xs = _ffn_call(xs, g[2], mod_s[2], w_in, w_out, (i, 1), 1, n_s)

    y_prompt = xp.reshape(bsz, seq, d)
    y_sample = xs.reshape(nt, nb, d).transpose(1, 0, 2)
    st = lambda k: jnp.stack(outs[k])
    return (y_prompt, y_sample, st("lat_p"), st("kr_p"), st("lat_s"), st("kr_s"),
            st("hre_p"), st("him_p"), st("hre_s"), st("him_s"))
```
